```python
import math
import jax
import jax.numpy as jnp
from jax import lax
import numpy as np

D_MODEL = 1024
BATCH = 32
SEQ = 2048
DEPTH = 4

GRID_W = 64
CTX_LEN = 256
EPS = 1e-6
D_MIX = D_MODEL
POOL_WINDOWS = (2, 4, 8, 16)
POOL_WIDTH = D_MIX // 4
POOL_GROUP = POOL_WIDTH // len(POOL_WINDOWS)
SSM_WIDTH = D_MIX // 4
SSM_CH = 16
SSM_GROUPS = SSM_WIDTH // SSM_CH
SSM_STATE = 64
HEAD_DIM = 64
ATTN_WIDTH = D_MIX - POOL_WIDTH - SSM_WIDTH
N_Q_HEADS = ATTN_WIDTH // HEAD_DIM
N_KV_HEADS = 2
Q_PER_KV = N_Q_HEADS // N_KV_HEADS
KV_WIDTH = N_KV_HEADS * HEAD_DIM
WINDOW = 128
BLOCK = 128
ROPE_BASE = 10000.0
D_FF = 2816
CONV_W = 3
D_IN = POOL_WIDTH + SSM_WIDTH + ATTN_WIDTH + 2 * KV_WIDTH
NEG_INF = -1e30

kernel_name = "hybrid_pool_s5_swa_dit_trunk"


def rms_norm(x, g):
    xf = x.astype(jnp.float32)
    y = xf * lax.rsqrt(jnp.mean(xf * xf, axis=-1, keepdims=True) + EPS)
    return (y * g.astype(jnp.float32)).astype(x.dtype)


def split_in(z):
    o1 = POOL_WIDTH
    o2 = o1 + SSM_WIDTH
    o3 = o2 + ATTN_WIDTH
    o4 = o3 + KV_WIDTH
    return z[..., :o1], z[..., o1:o2], z[..., o2:o3], z[..., o3:o4], z[..., o4:]


def pool_mixer(u, pool_w, pool_scale):
    B, L, _ = u.shape
    uf = u.astype(jnp.float32)
    cs = jnp.pad(jnp.cumsum(uf, axis=1), ((0, 0), (1, 0), (0, 0)))
    t = jnp.arange(L)
    groups = []
    for gi, w in enumerate(POOL_WINDOWS):
        sl = slice(gi * POOL_GROUP, (gi + 1) * POOL_GROUP)
        lo = jnp.clip(t - w // 2, 0, L)
        hi = jnp.clip(t - w // 2 + w, 0, L)
        win_sum = cs[:, hi, sl] - cs[:, lo, sl]
        cnt = (hi - lo).astype(jnp.float32)[None, :, None]
        groups.append(win_sum / cnt - uf[:, :, sl])
    pooled = jnp.stack(groups, axis=2)
    mixed = jnp.einsum('blgc,gcd->blgd', pooled, pool_w.astype(jnp.float32))
    return (mixed.reshape(B, L, POOL_WIDTH) * pool_scale.astype(jnp.float32)).astype(u.dtype)


def zoh(lam_re, lam_im, log_dt, b_re, b_im):
    lam = lax.complex(jnp.minimum(lam_re.astype(jnp.float32), -1e-4), lam_im.astype(jnp.float32))
    dt = jnp.exp(log_dt.astype(jnp.float32))[:, None]
    lam_bar = jnp.exp(lam * dt)
    b = lax.complex(b_re.astype(jnp.float32), b_im.astype(jnp.float32))
    b_bar = ((lam_bar - 1.0) / lam)[..., None] * b
    return lam_bar, b_bar


def _scan_op(e1, e2):
    a1, b1 = e1
    a2, b2 = e2
    return a1 * a2, a2 * b1 + b2


def diag_scan(lam_bar, bu, h0, reverse):
    first = -1 if reverse else 0
    bu = bu.at[:, first].add(lam_bar * h0)
    a = jnp.broadcast_to(lam_bar, bu.shape)
    _, h = lax.associative_scan(_scan_op, (a, bu), reverse=reverse, axis=1)
    return h


def s5_mixer(u_lat, u_ctx, lam_re, lam_im, log_dt, b_re, b_im, c_re, c_im, s5_d, w_glu, b_glu, need_ctx):
    B, L, _ = u_lat.shape
    N = u_ctx.shape[1]
    ul = u_lat.astype(jnp.float32).reshape(B, L, SSM_GROUPS, SSM_CH)
    uc = u_ctx.astype(jnp.float32).reshape(B, N, SSM_GROUPS, SSM_CH)
    zero_state = jnp.zeros((B, SSM_GROUPS, SSM_STATE), jnp.complex64)
    ys_lat, ys_ctx = [], []
    for d, reverse in enumerate((False, True)):
        lam_bar, b_bar = zoh(lam_re[d], lam_im[d], log_dt[d], b_re[d], b_im[d])
        cmat = lax.complex(c_re[d].astype(jnp.float32), c_im[d].astype(jnp.float32))
        h_ctx = diag_scan(lam_bar, jnp.einsum('bngh,gph->bngp', uc, b_bar), zero_state, reverse)
        h_carry = h_ctx[:, 0] if reverse else h_ctx[:, -1]
        h_lat = diag_scan(lam_bar, jnp.einsum('blgh,gph->blgp', ul, b_bar), h_carry, reverse)
        ys_lat.append(jnp.einsum('blgp,ghp->blgh', h_lat, cmat).real)
        if need_ctx:
            ys_ctx.append(jnp.einsum('bngp,ghp->bngh', h_ctx, cmat).real)
    dskip = s5_d.astype(jnp.float32).reshape(SSM_GROUPS, SSM_CH)

    def glu(y, u):
        y = (y + dskip * u).reshape(u.shape[0], u.shape[1], SSM_WIDTH)
        g = jax.nn.gelu(y)
        return (g * jax.nn.sigmoid(g @ w_glu.astype(jnp.float32) + b_glu.astype(jnp.float32))).astype(u_lat.dtype)

    out_lat = glu(ys_lat[0] + ys_lat[1], ul)
    out_ctx = glu(ys_ctx[0] + ys_ctx[1], uc) if need_ctx else None
    return out_lat, out_ctx


def axial_rope(x, row_pos, col_pos):
    half = HEAD_DIM // 2
    quarter = half // 2
    inv_freq = ROPE_BASE ** (-jnp.arange(quarter, dtype=jnp.float32) / quarter)

    def rotate(xa, pos):
        ang = pos.astype(jnp.float32)[:, None] * inv_freq
        cos = jnp.cos(ang)[None, :, None, :]
        sin = jnp.sin(ang)[None, :, None, :]
        x1, x2 = xa[..., :quarter], xa[..., quarter:]
        return jnp.concatenate([x1 * cos - x2 * sin, x1 * sin + x2 * cos], axis=-1)

    xf = x.astype(jnp.float32)
    out = jnp.concatenate([rotate(xf[..., :half], row_pos), rotate(xf[..., half:], col_pos)], axis=-1)
    return out.astype(x.dtype)


def windowed_attention(q, k, v, k_ctx, v_ctx, sink):
    B, L = q.shape[:2]
    N = k_ctx.shape[1]
    nb = L // BLOCK
    scale = HEAD_DIM ** -0.5
    qb = q.reshape(B, nb, BLOCK, N_KV_HEADS, Q_PER_KV, HEAD_DIM)
    pad = ((0, 0), (BLOCK, BLOCK), (0, 0), (0, 0))
    kp = jnp.pad(k, pad).reshape(B, nb + 2, BLOCK, N_KV_HEADS, HEAD_DIM)
    vp = jnp.pad(v, pad).reshape(B, nb + 2, BLOCK, N_KV_HEADS, HEAD_DIM)
    kw = jnp.concatenate([kp[:, :-2], kp[:, 1:-1], kp[:, 2:]], axis=2)
    vw = jnp.concatenate([vp[:, :-2], vp[:, 1:-1], vp[:, 2:]], axis=2)
    s_win = jnp.einsum('bnqkgd,bnjkd->bnkgqj', qb, kw, preferred_element_type=jnp.float32) * scale
    qpos = jnp.arange(nb)[:, None] * BLOCK + jnp.arange(BLOCK)[None, :]
    kpos = jnp.arange(nb)[:, None] * BLOCK - BLOCK + jnp.arange(3 * BLOCK)[None, :]
    valid = (jnp.abs(qpos[:, :, None] - kpos[:, None, :]) <= WINDOW) & (kpos[:, None, :] >= 0) & (kpos[:, None, :] < L)
    s_win = jnp.where(valid[None, :, None, None], s_win, NEG_INF)
    s_ctx = jnp.einsum('bnqkgd,bmkd->bnkgqm', qb, k_ctx, preferred_element_type=jnp.float32) * scale
    s_sink = jnp.broadcast_to(sink.astype(jnp.float32).reshape(N_KV_HEADS, Q_PER_KV)[None, None, :, :, None, None],
                              (B, nb, N_KV_HEADS, Q_PER_KV, BLOCK, 1))
    p = jax.nn.softmax(jnp.concatenate([s_sink, s_ctx, s_win], axis=-1), axis=-1)
    p_ctx = p[..., 1:1 + N].astype(v.dtype)
    p_win = p[..., 1 + N:].astype(v.dtype)
    out = (jnp.einsum('bnkgqm,bmkd->bnqkgd', p_ctx, v_ctx)
           + jnp.einsum('bnkgqj,bnjkd->bnqkgd', p_win, vw))
    return out.reshape(B, L, ATTN_WIDTH)


def context_attention(q, k, v, sink):
    B, N = q.shape[:2]
    scale = HEAD_DIM ** -0.5
    qg = q.reshape(B, N, N_KV_HEADS, Q_PER_KV, HEAD_DIM)
    s = jnp.einsum('bqkgd,bmkd->bkgqm', qg, k, preferred_element_type=jnp.float32) * scale
    s_sink = jnp.broadcast_to(sink.astype(jnp.float32).reshape(N_KV_HEADS, Q_PER_KV)[None, :, :, None, None],
                              (B, N_KV_HEADS, Q_PER_KV, N, 1))
    p = jax.nn.softmax(jnp.concatenate([s_sink, s], axis=-1), axis=-1)[..., 1:].astype(v.dtype)
    out = jnp.einsum('bkgqm,bmkd->bqkgd', p, v)
    return out.reshape(B, N, ATTN_WIDTH)


def token_mixing(h_lat, h_ctx, w_in, pool_w, pool_scale, lam_re, lam_im, log_dt, b_re, b_im,
                 c_re, c_im, s5_d, w_glu, b_glu, sink, w_out, need_ctx):
    B, L, _ = h_lat.shape
    N = h_ctx.shape[1]
    pl, sl, ql, kl, vl = split_in(h_lat @ w_in)
    pc, sc, qc, kc, vc = split_in(h_ctx @ w_in)
    a_lat = pool_mixer(pl, pool_w, pool_scale)
    b_lat, b_ctx = s5_mixer(sl, sc, lam_re, lam_im, log_dt, b_re, b_im, c_re, c_im, s5_d, w_glu, b_glu, need_ctx)
    rows = L // GRID_W
    row_pos = jnp.repeat(jnp.arange(rows), GRID_W)
    col_pos = jnp.tile(jnp.arange(GRID_W), rows)
    q_lat = axial_rope(ql.reshape(B, L, N_Q_HEADS, HEAD_DIM), row_pos, col_pos)
    k_lat = axial_rope(kl.reshape(B, L, N_KV_HEADS, HEAD_DIM), row_pos, col_pos)
    v_lat = vl.reshape(B, L, N_KV_HEADS, HEAD_DIM)
    k_ctx = kc.reshape(B, N, N_KV_HEADS, HEAD_DIM)
    v_ctx = vc.reshape(B, N, N_KV_HEADS, HEAD_DIM)
    att_lat = windowed_attention(q_lat, k_lat, v_lat, k_ctx, v_ctx, sink)
    y_lat = jnp.concatenate([a_lat, b_lat, att_lat], axis=-1) @ w_out
    if not need_ctx:
        return y_lat, None
    a_ctx = pool_mixer(pc, pool_w, pool_scale)
    att_ctx = context_attention(qc.reshape(B, N, N_Q_HEADS, HEAD_DIM), k_ctx, v_ctx, sink)
    y_ctx = jnp.concatenate([a_ctx, b_ctx, att_ctx], axis=-1) @ w_out
    return y_lat, y_ctx


def conv_ffn(h, w_up, conv_w, conv_b, w_down):
    u = h @ w_up
    u = lax.conv_general_dilated(u, conv_w[:, None, :], window_strides=(1,),
                                 padding=((CONV_W // 2, CONV_W // 2),),
                                 dimension_numbers=('NWC', 'WIO', 'NWC'),
                                 feature_group_count=2 * D_FF) + conv_b
    val, gate = jnp.split(u, 2, axis=-1)
    return (jax.nn.silu(gate) * val) @ w_down


def setup_inputs(seed: int = 0) -> dict:
    key = jax.random.key(seed)
    ks = jax.random.split(key, 32)
    f32 = jnp.float32

    def nrm(k, shape, s):
        return s * jax.random.normal(k, shape, f32)

    Ld = DEPTH
    G, P, H = SSM_GROUPS, SSM_STATE, SSM_CH
    n = jnp.arange(P, dtype=f32)
    return {
        "x": nrm(ks[0], (BATCH, SEQ, D_MODEL), 1.0),
        "c": nrm(ks[1], (BATCH, D_MODEL), 1.0),
        "ctx": nrm(ks[2], (BATCH, CTX_LEN, D_MODEL), 1.0),
        "c_ctx": nrm(ks[3], (D_MODEL,), 1.0),
        "w_ada": nrm(ks[4], (Ld, D_MODEL, 6 * D_MODEL), 0.5 * D_MODEL ** -0.5),
        "b_ada": nrm(ks[5], (Ld, 6 * D_MODEL), 0.02),
        "g_pre_mix": 1.0 + nrm(ks[6], (Ld, D_MODEL), 0.05),
        "g_post_mix": 1.0 + nrm(ks[7], (Ld, D_MODEL), 0.05),
        "g_pre_ffn": 1.0 + nrm(ks[8], (Ld, D_MODEL), 0.05),
        "g_post_ffn": 1.0 + nrm(ks[9], (Ld, D_MODEL), 0.05),
        "w_in": nrm(ks[10], (Ld, D_MODEL, D_IN), D_MODEL ** -0.5),
        "pool_w": nrm(ks[11], (Ld, len(POOL_WINDOWS), POOL_GROUP, POOL_GROUP), POOL_GROUP ** -0.5),
        "pool_scale": 1.0 + nrm(ks[12], (Ld, POOL_WIDTH), 0.1),
        "lam_re": -0.5 + nrm(ks[13], (Ld, 2, G, P), 0.01),
        "lam_im": math.pi * n + nrm(ks[14], (Ld, 2, G, P), 0.01),
        "log_dt": jax.random.uniform(ks[15], (Ld, 2, G), f32, math.log(1e-3), math.log(1e-1)),
        "b_re": nrm(ks[16], (Ld, 2, G, P, H), (2 * H) ** -0.5),
        "b_im": nrm(ks[17], (Ld, 2, G, P, H), (2 * H) ** -0.5),
        "c_re": nrm(ks[18], (Ld, 2, G, H, P), P ** -0.5),
        "c_im": nrm(ks[19], (Ld, 2, G, H, P), P ** -0.5),
        "s5_d": nrm(ks[20], (Ld, SSM_WIDTH), 1.0),
        "w_glu": nrm(ks[21], (Ld, SSM_WIDTH, SSM_WIDTH), SSM_WIDTH ** -0.5),
        "b_glu": nrm(ks[22], (Ld, SSM_WIDTH), 0.02),
        "sink": nrm(ks[23], (Ld, N_Q_HEADS), 0.5),
        "w_out": nrm(ks[24], (Ld, D_MIX, D_MODEL), D_MIX ** -0.5),
        "w_up": nrm(ks[25], (Ld, D_MODEL, 2 * D_FF), D_MODEL ** -0.5),
        "conv_w": nrm(ks[26], (Ld, CONV_W, 2 * D_FF), CONV_W ** -0.5),
        "conv_b": nrm(ks[27], (Ld, 2 * D_FF), 0.02),
        "w_down": nrm(ks[28], (Ld, D_FF, D_MODEL), D_FF ** -0.5),
    }


def reference(x, c, ctx, c_ctx, w_ada, b_ada, g_pre_mix, g_post_mix, g_pre_ffn, g_post_ffn,
              w_in, pool_w, pool_scale, lam_re, lam_im, log_dt, b_re, b_im, c_re, c_im,
              s5_d, w_glu, b_glu, sink, w_out, w_up, conv_w, conv_b, w_down):
    cond_lat = jax.nn.silu(c)
    cond_ctx = jax.nn.silu(c_ctx)
    ctx_s = ctx
    for l in range(DEPTH):
        need_ctx = l < DEPTH - 1
        m_lat = (cond_lat @ w_ada[l] + b_ada[l])[:, None, :]
        m_ctx = (cond_ctx @ w_ada[l] + b_ada[l])[None, None, :]
        sh1, sc1, gt1, sh2, sc2, gt2 = jnp.split(m_lat, 6, axis=-1)
        csh1, csc1, cgt1, csh2, csc2, cgt2 = jnp.split(m_ctx, 6, axis=-1)
        h_lat = rms_norm(x, g_pre_mix[l]) * (1.0 + sc1) + sh1
        h_ctx = rms_norm(ctx_s, g_pre_mix[l]) * (1.0 + csc1) + csh1
        y_lat, y_ctx = token_mixing(h_lat, h_ctx, w_in[l], pool_w[l], pool_scale[l], lam_re[l], lam_im[l],
                                    log_dt[l], b_re[l], b_im[l], c_re[l], c_im[l], s5_d[l], w_glu[l],
                                    b_glu[l], sink[l], w_out[l], need_ctx)
        x = x + gt1 * rms_norm(y_lat, g_post_mix[l])
        h_lat = rms_norm(x, g_pre_ffn[l]) * (1.0 + sc2) + sh2
        x = x + gt2 * rms_norm(conv_ffn(h_lat, w_up[l], conv_w[l], conv_b[l], w_down[l]), g_post_ffn[l])
        if need_ctx:
            ctx_s = ctx_s + cgt1 * rms_norm(y_ctx, g_post_mix[l])
            h_c = rms_norm(ctx_s, g_pre_ffn[l]) * (1.0 + csc2) + csh2
            ctx_s = ctx_s + cgt2 * rms_norm(conv_ffn(h_c, w_up[l], conv_w[l], conv_b[l], w_down[l]), g_post_ffn[l])
    return x
```

```python
import functools
import math

import jax
import jax.numpy as jnp
from jax import lax
from jax.experimental import pallas as pl
from jax.experimental.pallas import tpu as pltpu

F32 = jnp.float32
BF16 = jnp.bfloat16

D_MODEL = 1024
DEPTH = 4
GRID_W = 64
EPS = 1e-6
POOL_WINDOWS = (2, 4, 8, 16)
POOL_WIDTH = 256
POOL_GROUP = 64
SSM_WIDTH = 256
SSM_CH = 16
SSM_GROUPS = 16
SSM_STATE = 64
SSM_LANES = SSM_GROUPS * SSM_STATE
HEAD_DIM = 64
ATTN_WIDTH = 512
N_Q_HEADS = 8
N_KV_HEADS = 2
Q_PER_KV = 4
KV_WIDTH = 128
WINDOW = 128
BLOCK = 128
ROPE_BASE = 10000.0
D_FF = 2816
D_IN = 1280
NEG_INF = -1e30

LANE = 128
POOL_PAD = 16
POOL_HALO = 8
POOL_CHUNK = 128
TOK_TILE = 256
S5_STEPS = 16
FF_CHUNK = 256
HALO_ROWS = 16
VMEM_LIMIT = 56 * 1024 * 1024


def _cparams(sem):
    return pltpu.CompilerParams(dimension_semantics=sem, vmem_limit_bytes=VMEM_LIMIT)


def _rms(x, g):
    return x * lax.rsqrt(jnp.mean(x * x, axis=-1, keepdims=True) + EPS) * g


def _sigmoid(x):
    return 1.0 / (1.0 + jnp.exp(-x))


def _gelu_tanh(x):
    c = math.sqrt(2.0 / math.pi)
    return 0.5 * x * (1.0 + jnp.tanh(c * (x + 0.044715 * (x * x * x))))


def _ada_kernel(c_ref, w_ref, b_ref, o_ref):
    c = c_ref[...]
    s = c * _sigmoid(c)
    o_ref[...] = jnp.dot(s.astype(BF16), w_ref[...].astype(BF16), preferred_element_type=F32) + b_ref[...]


def _ada(cond, w_ada, b_ada):
    rows = cond.shape[0]
    nblk = w_ada.shape[-1] // D_MODEL
    return pl.pallas_call(
        _ada_kernel,
        out_shape=jax.ShapeDtypeStruct((DEPTH, rows, nblk * D_MODEL), F32),
        grid=(DEPTH, nblk),
        in_specs=[
            pl.BlockSpec((rows, D_MODEL), lambda l, j: (0, 0)),
            pl.BlockSpec((None, D_MODEL, D_MODEL), lambda l, j: (l, 0, j)),
            pl.BlockSpec((None, 1, D_MODEL), lambda l, j: (l, 0, j)),
        ],
        out_specs=pl.BlockSpec((None, rows, D_MODEL), lambda l, j: (l, 0, j)),
        compiler_params=_cparams(("arbitrary", "arbitrary")),
        name="ada_mod",
    )(cond, w_ada, b_ada.reshape(DEPTH, 1, -1))


def _in_proj_kernel(x_ref, mod_ref, g_ref, w_ref, cos_ref, sin_ref,
                    pool_ref, us_ref, q_ref, k_ref, v_ref):
    x = x_ref[...]
    h = _rms(x, g_ref[...]) * (1.0 + mod_ref[1:2, :]) + mod_ref[0:1, :]
    z = jnp.dot(h.astype(BF16), w_ref[...], preferred_element_type=F32)
    pool_ref[...] = z[:, 0:POOL_WIDTH]
    us_ref[...] = z[:, POOL_WIDTH:POOL_WIDTH + SSM_WIDTH]
    cos = cos_ref[...]
    sin = sin_ref[...]
    lane = lax.broadcasted_iota(jnp.int32, cos.shape, 1)
    first_half = (lane % 32) < 16
    q0 = POOL_WIDTH + SSM_WIDTH

    def rope(t):
        partner = jnp.where(first_half, pltpu.roll(t, LANE - 16, axis=1), pltpu.roll(t, 16, axis=1))
        return t * cos + partner * sin

    scale = HEAD_DIM ** -0.5
    for j in range(ATTN_WIDTH // LANE):
        t = z[:, q0 + j * LANE:q0 + (j + 1) * LANE]
        q_ref[:, j * LANE:(j + 1) * LANE] = (rope(t) * scale).astype(BF16)
    k0 = q0 + ATTN_WIDTH
    k_ref[...] = rope(z[:, k0:k0 + KV_WIDTH]).astype(BF16)
    v_ref[...] = z[:, k0 + KV_WIDTH:k0 + 2 * KV_WIDTH].astype(BF16)


def _in_proj(xc, mod, g, w_in, cos_t, sin_t, nct):
    B, S, _ = xc.shape
    T = TOK_TILE
    grid = (B, S // T)
    tok = lambda width: pl.BlockSpec((None, T, width), lambda b, i: (b, i, 0))
    return pl.pallas_call(
        _in_proj_kernel,
        out_shape=(
            jax.ShapeDtypeStruct((B, S, POOL_WIDTH), F32),
            jax.ShapeDtypeStruct((S, B * SSM_WIDTH), F32),
            jax.ShapeDtypeStruct((B, S, ATTN_WIDTH), BF16),
            jax.ShapeDtypeStruct((B, S, KV_WIDTH), BF16),
            jax.ShapeDtypeStruct((B, S, KV_WIDTH), BF16),
        ),
        grid=grid,
        in_specs=[
            tok(D_MODEL),
            pl.BlockSpec((None, None, 6, D_MODEL), lambda b, i: (b, jnp.where(i >= nct, 1, 0), 0, 0)),
            pl.BlockSpec((1, D_MODEL), lambda b, i: (0, 0)),
            pl.BlockSpec((D_MODEL, D_IN), lambda b, i: (0, 0)),
            pl.BlockSpec((T, LANE), lambda b, i: (i, 0)),
            pl.BlockSpec((T, LANE), lambda b, i: (i, 0)),
        ],
        out_specs=(
            tok(POOL_WIDTH),
            pl.BlockSpec((T, SSM_WIDTH), lambda b, i: (i, b)),
            tok(ATTN_WIDTH),
            tok(KV_WIDTH),
            tok(KV_WIDTH),
        ),
        compiler_params=_cparams(("parallel", "parallel")),
        name="in_proj",
    )(xc, mod, g, w_in, cos_t, sin_t)


def _pool_kernel(u_ref, w_ref, sc_ref, o_ref, pad_ref, *, n_ctx, n_lat):
    C = POOL_CHUNK
    zeros = jnp.zeros((POOL_PAD, POOL_WIDTH), F32)
    ctx0 = POOL_PAD
    lat0 = 2 * POOL_PAD + n_ctx
    pad_ref[0:POOL_PAD, :] = zeros
    pad_ref[ctx0 + n_ctx:lat0, :] = zeros
    pad_ref[lat0 + n_lat:lat0 + n_lat + POOL_PAD, :] = zeros
    pad_ref[ctx0:ctx0 + n_ctx, :] = u_ref[0:n_ctx, :]
    pad_ref[lat0:lat0 + n_lat, :] = u_ref[n_ctx:n_ctx + n_lat, :]

    t_idx = lax.broadcasted_iota(jnp.int32, (C, C + 2 * POOL_HALO), 0)
    s_idx = lax.broadcasted_iota(jnp.int32, (C, C + 2 * POOL_HALO), 1) - POOL_HALO
    bands = []
    for w in POOL_WINDOWS:
        lo = t_idx - w // 2
        bands.append(jnp.where((s_idx >= lo) & (s_idx < lo + w), 1.0, 0.0).astype(BF16))
    lane = lax.broadcasted_iota(jnp.int32, (C, POOL_WIDTH), 1)
    row = lax.broadcasted_iota(jnp.int32, (C, POOL_WIDTH), 0)
    wmat = w_ref[...]
    scale = sc_ref[...]

    def chunk(c, pad_base, out_base, seq_len):
        base = pl.multiple_of(pad_base + c * C, 8)
        win = pad_ref[pl.ds(base - POOL_HALO, C + 2 * POOL_HALO), :]
        hi = win.astype(BF16)
        lo = (win - hi.astype(F32)).astype(BF16)
        u = pad_ref[pl.ds(base, C), :]
        pos = row + c * C
        pooled = jnp.zeros((C, POOL_WIDTH), F32)
        for gi, w in enumerate(POOL_WINDOWS):
            s = (jnp.dot(bands[gi], hi, preferred_element_type=F32)
                 + jnp.dot(bands[gi], lo, preferred_element_type=F32))
            start = pos - w // 2
            cnt = jnp.minimum(start + w, seq_len) - jnp.maximum(start, 0)
            mean = s / cnt.astype(F32)
            pooled = jnp.where(lane // POOL_GROUP == gi, mean, pooled)
        pooled = pooled - u
        mixed = jnp.dot(pooled.astype(BF16), wmat, preferred_element_type=F32) * scale
        o_ref[pl.ds(pl.multiple_of(out_base + c * C, 8), C), :] = mixed.astype(BF16)

    def ctx_body(c, carry):
        chunk(c, ctx0, 0, n_ctx)
        return carry

    def lat_body(c, carry):
        chunk(c, lat0, n_ctx, n_lat)
        return carry

    lax.fori_loop(0, n_ctx // C, ctx_body, 0)
    lax.fori_loop(0, n_lat // C, lat_body, 0)


def _pool(pool_in, w_blk, scale, n_ctx):
    B, S, _ = pool_in.shape
    n_lat = S - n_ctx
    return pl.pallas_call(
        functools.partial(_pool_kernel, n_ctx=n_ctx, n_lat=n_lat),
        out_shape=jax.ShapeDtypeStruct((B, S, POOL_WIDTH), BF16),
        grid=(B,),
        in_specs=[
            pl.BlockSpec((None, S, POOL_WIDTH), lambda b: (b, 0, 0)),
            pl.BlockSpec((POOL_WIDTH, POOL_WIDTH), lambda b: (0, 0)),
            pl.BlockSpec((1, POOL_WIDTH), lambda b: (0, 0)),
        ],
        out_specs=pl.BlockSpec((None, S, POOL_WIDTH), lambda b: (b, 0, 0)),
        scratch_shapes=[pltpu.VMEM((S + 3 * POOL_PAD, POOL_WIDTH), F32)],
        compiler_params=_cparams(("parallel",)),
        name="pool_mixer",
    )(pool_in, w_blk, scale)


def _s5_scan_chunk(u_ref, bcat_ref, lam_ref, ccat_ref, hbuf, state, *, batch, reverse):
    @pl.when(pl.program_id(0) == 0)
    def _():
        state[...] = jnp.zeros(state.shape, F32)

    hbuf[...] = jnp.dot(u_ref[...].astype(BF16), bcat_ref[...], preferred_element_type=F32)
    for cb in range(SSM_LANES // LANE):
        re = slice(cb * LANE, (cb + 1) * LANE)
        im = slice(SSM_LANES + cb * LANE, SSM_LANES + (cb + 1) * LANE)
        lr = jnp.broadcast_to(lam_ref[0:1, re], (batch, LANE))
        li = jnp.broadcast_to(lam_ref[0:1, im], (batch, LANE))
        s_re = state[:, re]
        s_im = state[:, im]
        for step in range(S5_STEPS):
            t = S5_STEPS - 1 - step if reverse else step
            rows = slice(t * batch, (t + 1) * batch)
            n_re = lr * s_re - li * s_im + hbuf[rows, re]
            n_im = lr * s_im + li * s_re + hbuf[rows, im]
            hbuf[rows, re] = n_re
            hbuf[rows, im] = n_im
            s_re, s_im = n_re, n_im
        state[:, re] = s_re
        state[:, im] = s_im
    return jnp.dot(hbuf[...].astype(BF16), ccat_ref[...], preferred_element_type=F32)


def _s5_fwd_kernel(u_ref, bcat_ref, lam_ref, ccat_ref, y_ref, hbuf, state, *, batch):
    y_ref[...] = _s5_scan_chunk(u_ref, bcat_ref, lam_ref, ccat_ref, hbuf, state, batch=batch, reverse=False)


def _s5_bwd_kernel(u_ref, bcat_ref, lam_ref, ccat_ref, yf_ref, dsk_ref, wglu_ref, bglu_ref,
                   o_ref, hbuf, state, *, batch):
    y = _s5_scan_chunk(u_ref, bcat_ref, lam_ref, ccat_ref, hbuf, state, batch=batch, reverse=True)
    tot = yf_ref[...] + y + dsk_ref[...] * u_ref[...]
    g = _gelu_tanh(tot)
    gate = jnp.dot(g.astype(BF16), wglu_ref[...], preferred_element_type=F32) + bglu_ref[...]
    o_ref[...] = (g * _sigmoid(gate)).astype(BF16)


def _s5(us_tm, bcat, lam, ccat, dskip, w_glu, b_glu, batch, n_ctx):
    rows_total = us_tm.shape[0]
    R = S5_STEPS * batch
    nch = rows_total // R
    nctx = n_ctx // S5_STEPS
    const = lambda shape: pl.BlockSpec(shape, lambda j: (0,) * len(shape))
    per_dir = lambda shape: pl.BlockSpec((None,) + shape, None)

    def dir_spec(d, shape):
        return pl.BlockSpec((None,) + shape, lambda j: (d,) + (0,) * len(shape))

    scratch = [pltpu.VMEM((R, 2 * SSM_LANES), F32), pltpu.VMEM((batch, 2 * SSM_LANES), F32)]
    yf = pl.pallas_call(
        functools.partial(_s5_fwd_kernel, batch=batch),
        out_shape=jax.ShapeDtypeStruct((rows_total, SSM_WIDTH), F32),
        grid=(nch,),
        in_specs=[
            pl.BlockSpec((R, SSM_WIDTH), lambda j: (j, 0)),
            dir_spec(0, (SSM_WIDTH, 2 * SSM_LANES)),
            dir_spec(0, (1, 2 * SSM_LANES)),
            dir_spec(0, (2 * SSM_LANES, SSM_WIDTH)),
        ],
        out_specs=pl.BlockSpec((R, SSM_WIDTH), lambda j: (j, 0)),
        scratch_shapes=scratch,
        compiler_params=_cparams(("arbitrary",)),
        name="s5_forward",
    )(us_tm, bcat, lam, ccat)

    def rev(j):
        return (jnp.where(j < nctx, nctx - 1 - j, nch + nctx - 1 - j), 0)

    return pl.pallas_call(
        functools.partial(_s5_bwd_kernel, batch=batch),
        out_shape=jax.ShapeDtypeStruct((rows_total, SSM_WIDTH), BF16),
        grid=(nch,),
        in_specs=[
            pl.BlockSpec((R, SSM_WIDTH), rev),
            dir_spec(1, (SSM_WIDTH, 2 * SSM_LANES)),
            dir_spec(1, (1, 2 * SSM_LANES)),
            dir_spec(1, (2 * SSM_LANES, SSM_WIDTH)),
            pl.BlockSpec((R, SSM_WIDTH), rev),
            const((1, SSM_WIDTH)),
            const((SSM_WIDTH, SSM_WIDTH)),
            const((1, SSM_WIDTH)),
        ],
        out_specs=pl.BlockSpec((R, SSM_WIDTH), rev),
        scratch_shapes=scratch,
        compiler_params=_cparams(("arbitrary",)),
        name="s5_backward",
    )(us_tm, bcat, lam, ccat, yf, dskip, w_glu, b_glu)


def _attn_head(qm, keys, vals, masks, sink):
    scores = []
    for kk, mask in zip(keys, masks):
        s = lax.dot_general(qm, kk, (((1,), (1,)), ((), ())), preferred_element_type=F32)
        scores.append(s if mask is None else jnp.where(mask, s, NEG_INF))
    m = sink
    for s in scores:
        m = jnp.maximum(m, jnp.max(s, axis=-1, keepdims=True))
    den = jnp.exp(sink - m)
    acc = None
    for s, vv in zip(scores, vals):
        e = jnp.exp(s - m)
        den = den + jnp.sum(e, axis=-1, keepdims=True)
        pv = jnp.dot(e.astype(BF16), vv, preferred_element_type=F32)
        acc = pv if acc is None else acc + pv
    return acc / den


def _attn_kernel(q_ref, kc_ref, vc_ref, kp_ref, kq_ref, kn_ref, vp_ref, vq_ref, vn_ref, sink_ref,
                 o_ref, *, nct, n_lat, tile0):
    i = pl.program_id(1) + tile0
    lane = lax.broadcasted_iota(jnp.int32, (BLOCK, LANE), 1)
    low = lane < HEAD_DIM

    def run(keys, vals, masks):
        for p in range(ATTN_WIDTH // LANE):
            qp = q_ref[:, p * LANE:(p + 1) * LANE]
            zero = jnp.zeros_like(qp)
            o_lo = _attn_head(jnp.where(low, qp, zero), keys, vals, masks, sink_ref[p:p + 1, 0:1])
            o_hi = _attn_head(jnp.where(low, zero, qp), keys, vals, masks,
                              sink_ref[Q_PER_KV + p:Q_PER_KV + p + 1, 0:1])
            o_ref[:, p * LANE:(p + 1) * LANE] = jnp.where(low, o_lo, o_hi).astype(BF16)

    @pl.when(i < nct)
    def _():
        run([kc_ref[...]], [vc_ref[...]], [None])

    @pl.when(i >= nct)
    def _():
        li = i - nct
        qpos = li * BLOCK + lax.broadcasted_iota(jnp.int32, (BLOCK, 3 * BLOCK), 0)
        kpos = (li - 1) * BLOCK + lax.broadcasted_iota(jnp.int32, (BLOCK, 3 * BLOCK), 1)
        valid = (jnp.abs(qpos - kpos) <= WINDOW) & (kpos >= 0) & (kpos < n_lat)
        kw = jnp.concatenate([kp_ref[...], kq_ref[...], kn_ref[...]], axis=0)
        vw = jnp.concatenate([vp_ref[...], vq_ref[...], vn_ref[...]], axis=0)
        run([kc_ref[...], kw], [vc_ref[...], vw], [None, valid])


def _attention(q, k, v, sink_b, n_ctx, lat_only):
    B, S, _ = q.shape
    nct = n_ctx // BLOCK
    ntiles = S // BLOCK
    tile0 = nct if lat_only else 0
    rows_out = S - tile0 * BLOCK
    cur = lambda b, i: (b, i + tile0, 0)
    prev = lambda b, i: (b, jnp.maximum(i + tile0 - 1, nct), 0)
    nxt = lambda b, i: (b, jnp.minimum(i + tile0 + 1, ntiles - 1), 0)
    kv = lambda imap: pl.BlockSpec((None, BLOCK, KV_WIDTH), imap)
    ctx = pl.BlockSpec((None, n_ctx, KV_WIDTH), lambda b, i: (b, 0, 0))
    return pl.pallas_call(
        functools.partial(_attn_kernel, nct=nct, n_lat=S - n_ctx, tile0=tile0),
        out_shape=jax.ShapeDtypeStruct((B, rows_out, ATTN_WIDTH), BF16),
        grid=(B, ntiles - tile0),
        in_specs=[
            pl.BlockSpec((None, BLOCK, ATTN_WIDTH), cur),
            ctx, ctx,
            kv(prev), kv(cur), kv(nxt),
            kv(prev), kv(cur), kv(nxt),
            pl.BlockSpec((N_Q_HEADS, LANE), lambda b, i: (0, 0)),
        ],
        out_specs=pl.BlockSpec((None, BLOCK, ATTN_WIDTH), lambda b, i: (b, i, 0)),
        compiler_params=_cparams(("parallel", "parallel")),
        name="attention",
    )(q, k, v, k, k, k, v, v, v, sink_b)


def _out_proj_kernel(a_ref, b_ref, c_ref, x_ref, mod_ref, wa_ref, wb_ref, wc_ref, gpost_ref, gpre_ref,
                     x1_ref, h2_ref):
    y = (jnp.dot(a_ref[...], wa_ref[...], preferred_element_type=F32)
         + jnp.dot(b_ref[...], wb_ref[...], preferred_element_type=F32)
         + jnp.dot(c_ref[...], wc_ref[...], preferred_element_type=F32))
    x1 = x_ref[...] + mod_ref[2:3, :] * _rms(y, gpost_ref[...])
    x1_ref[...] = x1
    h2 = _rms(x1, gpre_ref[...]) * (1.0 + mod_ref[4:5, :]) + mod_ref[3:4, :]
    h2_ref[...] = h2.astype(BF16)


def _out_proj(a, bs, att, xc, mod, wa, wb, wc, gpost, gpre, n_ctx, lat_only):
    B, S, _ = xc.shape
    T = TOK_TILE
    nct = n_ctx // T
    tile0 = nct if lat_only else 0
    rows_out = S - tile0 * T
    full = lambda width: pl.BlockSpec((None, T, width), lambda b, i: (b, i + tile0, 0))
    outs = lambda width: pl.BlockSpec((None, T, width), lambda b, i: (b, i, 0))
    const = lambda shape: pl.BlockSpec(shape, lambda b, i: (0,) * len(shape))
    return pl.pallas_call(
        _out_proj_kernel,
        out_shape=(jax.ShapeDtypeStruct((B, rows_out, D_MODEL), F32),
                   jax.ShapeDtypeStruct((B, rows_out, D_MODEL), BF16)),
        grid=(B, S // T - tile0),
        in_specs=[
            full(POOL_WIDTH),
            pl.BlockSpec((T, SSM_WIDTH), lambda b, i: (i + tile0, b)),
            outs(ATTN_WIDTH),
            full(D_MODEL),
            pl.BlockSpec((None, None, 6, D_MODEL), lambda b, i: (b, jnp.where(i + tile0 >= nct, 1, 0), 0, 0)),
            const((POOL_WIDTH, D_MODEL)), const((SSM_WIDTH, D_MODEL)), const((ATTN_WIDTH, D_MODEL)),
            const((1, D_MODEL)), const((1, D_MODEL)),
        ],
        out_specs=(outs(D_MODEL), outs(D_MODEL)),
        compiler_params=_cparams(("parallel", "parallel")),
        name="out_proj",
    )(a, bs, att, xc, mod, wa, wb, wc, gpost, gpre)


def _ffn_kernel(h_ref, hp_ref, hn_ref, x_ref, mod_ref, wv_ref, wg_ref, cwv_ref, cwg_ref, cbv_ref, cbg_ref,
                wd_ref, g_ref, o_ref, acc_ref, *, seq_starts, seq_ends):
    i = pl.program_id(1)
    T = h_ref.shape[0]
    first = functools.reduce(jnp.logical_or, [i == s for s in seq_starts])
    last = functools.reduce(jnp.logical_or, [i == e for e in seq_ends])
    h = h_ref[...]
    halo = jnp.concatenate([hp_ref[...], hn_ref[...]], axis=0)
    row = lax.broadcasted_iota(jnp.int32, (T, FF_CHUNK), 0)
    acc_ref[...] = jnp.zeros(acc_ref.shape, F32)

    def conv(u, edge, cw, cb):
        prev_row = jnp.where(first, 0.0, edge[HALO_ROWS - 1:HALO_ROWS, :])
        next_row = jnp.where(last, 0.0, edge[HALO_ROWS:HALO_ROWS + 1, :])
        up = jnp.where(row == 0, prev_row, pltpu.roll(u, 1, axis=0))
        un = jnp.where(row == T - 1, next_row, pltpu.roll(u, T - 1, axis=0))
        return cw[0:1, :] * up + cw[1:2, :] * u + cw[2:3, :] * un + cb

    def body(j, carry):
        wv = wv_ref[j]
        wg = wg_ref[j]
        val = conv(jnp.dot(h, wv, preferred_element_type=F32),
                   jnp.dot(halo, wv, preferred_element_type=F32), cwv_ref[j], cbv_ref[j])
        gate = conv(jnp.dot(h, wg, preferred_element_type=F32),
                    jnp.dot(halo, wg, preferred_element_type=F32), cwg_ref[j], cbg_ref[j])
        act = gate * _sigmoid(gate) * val
        acc_ref[...] += jnp.dot(act.astype(BF16), wd_ref[j], preferred_element_type=F32)
        return carry

    lax.fori_loop(0, wv_ref.shape[0], body, 0)
    o_ref[...] = x_ref[...] + mod_ref[5:6, :] * _rms(acc_ref[...], g_ref[...])


def _ffn(h2, x1, mod, wv, wg, cwv, cwg, cbv, cbg, wd, g, n_ctx, lat_only):
    B, rows, _ = x1.shape
    T = TOK_TILE
    ntiles = rows // T
    nct = 0 if lat_only else n_ctx // T
    seq_starts = (0,) if lat_only else (0, nct)
    seq_ends = (ntiles - 1,) if lat_only else (nct - 1, ntiles - 1)
    hb = T // HALO_ROWS
    nhb = rows // HALO_ROWS
    nch = wv.shape[0]
    tok = lambda width: pl.BlockSpec((None, T, width), lambda b, i: (b, i, 0))
    const = lambda shape: pl.BlockSpec(shape, lambda b, i: (0,) * len(shape))
    return pl.pallas_call(
        functools.partial(_ffn_kernel, seq_starts=seq_starts, seq_ends=seq_ends),
        out_shape=jax.ShapeDtypeStruct((B, rows, D_MODEL), F32),
        grid=(B, ntiles),
        in_specs=[
            tok(D_MODEL),
            pl.BlockSpec((None, HALO_ROWS, D_MODEL), lambda b, i: (b, jnp.maximum(i * hb - 1, 0), 0)),
            pl.BlockSpec((None, HALO_ROWS, D_MODEL), lambda b, i: (b, jnp.minimum((i + 1) * hb, nhb - 1), 0)),
            tok(D_MODEL),
            pl.BlockSpec((None, None, 6, D_MODEL),
                         lambda b, i: (b, jnp.where(i >= nct, 1, 0), 0, 0)),
            const((nch, D_MODEL, FF_CHUNK)), const((nch, D_MODEL, FF_CHUNK)),
            const((nch, 3, FF_CHUNK)), const((nch, 3, FF_CHUNK)),
            const((nch, 1, FF_CHUNK)), const((nch, 1, FF_CHUNK)),
            const((nch, FF_CHUNK, D_MODEL)),
            const((1, D_MODEL)),
        ],
        out_specs=tok(D_MODEL),
        scratch_shapes=[pltpu.VMEM((T, D_MODEL), F32)],
        compiler_params=_cparams(("parallel", "parallel")),
        name="conv_ffn",
    )(h2, h2, h2, x1, mod, wv, wg, cwv, cwg, cbv, cbg, wd, g)


def _zoh_blocks(lam_re, lam_im, log_dt, b_re, b_im, c_re, c_im):
    lr = jnp.minimum(lam_re, -1e-4)
    li = lam_im
    dt = jnp.exp(log_dt)[..., None]
    mag = jnp.exp(lr * dt)
    lbr = mag * jnp.cos(li * dt)
    lbi = mag * jnp.sin(li * dt)
    den = lr * lr + li * li
    fr = ((lbr - 1.0) * lr + lbi * li) / den
    fi = (lbi * lr - (lbr - 1.0) * li) / den
    bbr = fr[..., None] * b_re - fi[..., None] * b_im
    bbi = fr[..., None] * b_im + fi[..., None] * b_re
    eye = jnp.eye(SSM_GROUPS, dtype=F32)
    blk_b = lambda t: jnp.einsum('dgph,gk->dghkp', t, eye).reshape(2, SSM_WIDTH, SSM_LANES)
    blk_c = lambda t: jnp.einsum('dghp,gk->dgpkh', t, eye).reshape(2, SSM_LANES, SSM_WIDTH)
    bcat = jnp.concatenate([blk_b(bbr), blk_b(bbi)], axis=-1).astype(BF16)
    ccat = jnp.concatenate([blk_c(c_re), -blk_c(c_im)], axis=1).astype(BF16)
    lam = jnp.concatenate([lbr.reshape(2, 1, SSM_LANES), lbi.reshape(2, 1, SSM_LANES)], axis=-1)
    return bcat, lam, ccat


def _head_perm():
    cols = []
    for p in range(Q_PER_KV):
        for h in (p, Q_PER_KV + p):
            cols.extend(range(h * HEAD_DIM, (h + 1) * HEAD_DIM))
    return jnp.array(cols, dtype=jnp.int32)


def _rope_tables(n_ctx, n_lat):
    t = jnp.arange(n_lat)
    lane = jnp.arange(LANE)
    hl = lane % HEAD_DIM
    quarter = HEAD_DIM // 4
    inv_freq = ROPE_BASE ** (-jnp.arange(quarter, dtype=F32) / quarter)
    pos = jnp.where(hl[None, :] < HEAD_DIM // 2, (t // GRID_W)[:, None], (t % GRID_W)[:, None])
    ang = pos.astype(F32) * inv_freq[hl % quarter][None, :]
    first_half = (hl % (2 * quarter)) < quarter
    cos_l = jnp.cos(ang)
    sin_l = jnp.where(first_half[None, :], -jnp.sin(ang), jnp.sin(ang))
    cos_t = jnp.concatenate([jnp.ones((n_ctx, LANE), F32), cos_l], axis=0)
    sin_t = jnp.concatenate([jnp.zeros((n_ctx, LANE), F32), sin_l], axis=0)
    return cos_t, sin_t


def kernel(x, c, ctx, c_ctx, w_ada, b_ada, g_pre_mix, g_post_mix, g_pre_ffn, g_post_ffn, w_in, pool_w, pool_scale, lam_re, lam_im, log_dt, b_re, b_im, c_re, c_im, s5_d, w_glu, b_glu, sink, w_out, w_up, conv_w, conv_b, w_down):
    B, L, _ = x.shape
    N = ctx.shape[1]
    S = N + L
    assert N % TOK_TILE == 0 and L % TOK_TILE == 0 and L % GRID_W == 0

    rows = -(-(B + 1) // 8) * 8
    cond = jnp.concatenate([c, c_ctx[None, :], jnp.zeros((rows - B - 1, D_MODEL), F32)], axis=0)
    mod_all = _ada(cond, w_ada, b_ada)
    mod_lat = mod_all[:, :B].reshape(DEPTH, B, 1, 6, D_MODEL)
    mod_ctx = jnp.broadcast_to(mod_all[:, B].reshape(DEPTH, 1, 1, 6, D_MODEL), (DEPTH, B, 1, 6, D_MODEL))
    mods = jnp.concatenate([mod_ctx, mod_lat], axis=2)

    cos_t, sin_t = _rope_tables(N, L)
    perm = _head_perm()
    q0 = POOL_WIDTH + SSM_WIDTH
    in_cols = jnp.concatenate([jnp.arange(q0), q0 + perm, jnp.arange(q0 + ATTN_WIDTH, D_IN)])
    out_rows = jnp.concatenate([jnp.arange(q0), q0 + perm])
    eye_w = jnp.eye(len(POOL_WINDOWS), dtype=F32)
    nch = D_FF // FF_CHUNK

    xc = jnp.concatenate([ctx, x], axis=1)
    for l in range(DEPTH):
        last = l == DEPTH - 1
        w_in_l = w_in[l][:, in_cols].astype(BF16)
        w_out_l = w_out[l][out_rows].astype(BF16)
        pool_blk = jnp.einsum('gcd,gk->gckd', pool_w[l], eye_w).reshape(POOL_WIDTH, POOL_WIDTH).astype(BF16)
        bcat, lam, ccat = _zoh_blocks(lam_re[l], lam_im[l], log_dt[l], b_re[l], b_im[l], c_re[l], c_im[l])
        sink_b = jnp.broadcast_to(sink[l][:, None], (N_Q_HEADS, LANE))
        wv = w_up[l][:, :D_FF].reshape(D_MODEL, nch, FF_CHUNK).transpose(1, 0, 2).astype(BF16)
        wg = w_up[l][:, D_FF:].reshape(D_MODEL, nch, FF_CHUNK).transpose(1, 0, 2).astype(BF16)
        cwv = conv_w[l][:, :D_FF].reshape(3, nch, FF_CHUNK).transpose(1, 0, 2)
        cwg = conv_w[l][:, D_FF:].reshape(3, nch, FF_CHUNK).transpose(1, 0, 2)
        cbv = conv_b[l][:D_FF].reshape(nch, 1, FF_CHUNK)
        cbg = conv_b[l][D_FF:].reshape(nch, 1, FF_CHUNK)
        wd = w_down[l].reshape(nch, FF_CHUNK, D_MODEL).astype(BF16)

        pool_in, us, q, k, v = _in_proj(xc, mods[l], g_pre_mix[l][None, :], w_in_l, cos_t, sin_t, N // TOK_TILE)
        a = _pool(pool_in, pool_blk, pool_scale[l][None, :], N)
        bs = _s5(us.reshape(S * B, SSM_WIDTH), bcat, lam, ccat, s5_d[l][None, :],
                 w_glu[l].astype(BF16), b_glu[l][None, :], B, N)
        att = _attention(q, k, v, sink_b, N, last)
        x1, h2 = _out_proj(a, bs.reshape(S, B * SSM_WIDTH), att, xc, mods[l],
                           w_out_l[:POOL_WIDTH], w_out_l[POOL_WIDTH:q0], w_out_l[q0:],
                           g_post_mix[l][None, :], g_pre_ffn[l][None, :], N, last)
        xc = _ffn(h2, x1, mods[l], wv, wg, cwv, cwg, cbv, cbg, wd, g_post_ffn[l][None, :], N, last)
    return xc
```

```python
import functools
import math

import jax
import jax.numpy as jnp
from jax import lax
from jax.experimental import pallas as pl
from jax.experimental.pallas import tpu as pltpu

F32 = jnp.float32
BF16 = jnp.bfloat16

D_MODEL = 1024
DEPTH = 4
GRID_W = 64
EPS = 1e-6
POOL_WINDOWS = (2, 4, 8, 16)
POOL_WIDTH = 256
POOL_GROUP = 64
SSM_WIDTH = 256
SSM_CH = 16
SSM_GROUPS = 16
SSM_STATE = 64
SSM_LANES = SSM_GROUPS * SSM_STATE
HEAD_DIM = 64
ATTN_WIDTH = 512
N_Q_HEADS = 8
N_KV_HEADS = 2
Q_PER_KV = 4
KV_WIDTH = 128
WINDOW = 128
BLOCK = 128
ROPE_BASE = 10000.0
D_FF = 2816
D_IN = 1280
NEG_INF = -1e30

LANE = 128
POOL_PAD = 16
POOL_HALO = 8
POOL_CHUNK = 128
TOK_TILE = 256
S5_STEPS = 16
FF_CHUNK = 256
HALO_ROWS = 16
SUBLANES = 8
VMEM_LIMIT = 56 * 1024 * 1024


def _cparams(sem):
    return pltpu.CompilerParams(dimension_semantics=sem, vmem_limit_bytes=VMEM_LIMIT)


def _rms(x, g):
    return x * lax.rsqrt(jnp.mean(x * x, axis=-1, keepdims=True) + EPS) * g


def _sigmoid(x):
    return 1.0 / (1.0 + jnp.exp(-x))


def _gelu_tanh(x):
    c = math.sqrt(2.0 / math.pi)
    return 0.5 * x * (1.0 + jnp.tanh(c * (x + 0.044715 * (x * x * x))))


def _ada_kernel(c_ref, w_ref, b_ref, o_ref):
    c = c_ref[...]
    s = c * _sigmoid(c)
    o_ref[...] = jnp.dot(s.astype(BF16), w_ref[...].astype(BF16), preferred_element_type=F32) + b_ref[...]


def _ada(cond, w_ada, b_ada):
    rows = cond.shape[0]
    nblk = w_ada.shape[-1] // D_MODEL
    return pl.pallas_call(
        _ada_kernel,
        out_shape=jax.ShapeDtypeStruct((DEPTH, rows, nblk * D_MODEL), F32),
        grid=(DEPTH, nblk),
        in_specs=[
            pl.BlockSpec((rows, D_MODEL), lambda l, j: (0, 0)),
            pl.BlockSpec((None, D_MODEL, D_MODEL), lambda l, j: (l, 0, j)),
            pl.BlockSpec((None, 1, D_MODEL), lambda l, j: (l, 0, j)),
        ],
        out_specs=pl.BlockSpec((None, rows, D_MODEL), lambda l, j: (l, 0, j)),
        compiler_params=_cparams(("arbitrary", "arbitrary")),
        name="ada_mod",
    )(cond, w_ada, b_ada.reshape(DEPTH, 1, -1))


def _in_proj_kernel(x_ref, mod_ref, g_ref, w_ref, cos_ref, sin_ref,
                    pool_ref, us_ref, q_ref, kt_ref, v_ref):
    x = x_ref[...]
    h = _rms(x, g_ref[...]) * (1.0 + mod_ref[1:2, :]) + mod_ref[0:1, :]
    z = jnp.dot(h.astype(BF16), w_ref[...], preferred_element_type=F32)
    pool_ref[...] = z[:, 0:POOL_WIDTH]
    us_ref[...] = z[:, POOL_WIDTH:POOL_WIDTH + SSM_WIDTH]
    cos = cos_ref[...]
    sin = sin_ref[...]
    lane = lax.broadcasted_iota(jnp.int32, cos.shape, 1)
    first_half = (lane % 32) < 16
    q0 = POOL_WIDTH + SSM_WIDTH

    def rope(t):
        partner = jnp.where(first_half, pltpu.roll(t, LANE - 16, axis=1), pltpu.roll(t, 16, axis=1))
        return t * cos + partner * sin

    scale = HEAD_DIM ** -0.5
    for j in range(ATTN_WIDTH // LANE):
        t = z[:, q0 + j * LANE:q0 + (j + 1) * LANE]
        q_ref[:, j * LANE:(j + 1) * LANE] = (rope(t) * scale).astype(BF16)
    k0 = q0 + ATTN_WIDTH
    kt_ref[...] = rope(z[:, k0:k0 + KV_WIDTH]).T.astype(BF16)
    v_ref[...] = z[:, k0 + KV_WIDTH:k0 + 2 * KV_WIDTH].astype(BF16)


def _in_proj(xc, mod, g, w_in, cos_t, sin_t, nct):
    B, S, _ = xc.shape
    T = TOK_TILE
    grid = (B, S // T)
    tok = lambda width: pl.BlockSpec((None, T, width), lambda b, i: (b, i, 0))
    return pl.pallas_call(
        _in_proj_kernel,
        out_shape=(
            jax.ShapeDtypeStruct((B, S, POOL_WIDTH), F32),
            jax.ShapeDtypeStruct((S, B * SSM_WIDTH), F32),
            jax.ShapeDtypeStruct((B, S, ATTN_WIDTH), BF16),
            jax.ShapeDtypeStruct((B, KV_WIDTH, S), BF16),
            jax.ShapeDtypeStruct((B, S, KV_WIDTH), BF16),
        ),
        grid=grid,
        in_specs=[
            tok(D_MODEL),
            pl.BlockSpec((None, None, 6, D_MODEL), lambda b, i: (b, jnp.where(i >= nct, 1, 0), 0, 0)),
            pl.BlockSpec((1, D_MODEL), lambda b, i: (0, 0)),
            pl.BlockSpec((D_MODEL, D_IN), lambda b, i: (0, 0)),
            pl.BlockSpec((T, LANE), lambda b, i: (i, 0)),
            pl.BlockSpec((T, LANE), lambda b, i: (i, 0)),
        ],
        out_specs=(
            tok(POOL_WIDTH),
            pl.BlockSpec((T, SSM_WIDTH), lambda b, i: (i, b)),
            tok(ATTN_WIDTH),
            pl.BlockSpec((None, KV_WIDTH, T), lambda b, i: (b, 0, i)),
            tok(KV_WIDTH),
        ),
        compiler_params=_cparams(("parallel", "parallel")),
        name="in_proj",
    )(xc, mod, g, w_in, cos_t, sin_t)


def _pool_kernel(u_ref, w_ref, sc_ref, o_ref, pad_ref, *, n_ctx, n_lat):
    C = POOL_CHUNK
    zeros = jnp.zeros((POOL_PAD, POOL_WIDTH), F32)
    ctx0 = POOL_PAD
    lat0 = 2 * POOL_PAD + n_ctx
    pad_ref[0:POOL_PAD, :] = zeros
    pad_ref[ctx0 + n_ctx:lat0, :] = zeros
    pad_ref[lat0 + n_lat:lat0 + n_lat + POOL_PAD, :] = zeros
    pad_ref[ctx0:ctx0 + n_ctx, :] = u_ref[0:n_ctx, :]
    pad_ref[lat0:lat0 + n_lat, :] = u_ref[n_ctx:n_ctx + n_lat, :]

    t_idx = lax.broadcasted_iota(jnp.int32, (C, C + 2 * POOL_HALO), 0)
    s_idx = lax.broadcasted_iota(jnp.int32, (C, C + 2 * POOL_HALO), 1) - POOL_HALO
    bands = []
    for w in POOL_WINDOWS:
        lo = t_idx - w // 2
        bands.append(jnp.where((s_idx >= lo) & (s_idx < lo + w), 1.0, 0.0).astype(BF16))
    lane = lax.broadcasted_iota(jnp.int32, (C, POOL_WIDTH), 1)
    row = lax.broadcasted_iota(jnp.int32, (C, POOL_WIDTH), 0)
    wmat = w_ref[...]
    scale = sc_ref[...]

    def chunk(c, pad_base, out_base, seq_len):
        base = pl.multiple_of(pad_base + c * C, 8)
        win = pad_ref[pl.ds(base - POOL_HALO, C + 2 * POOL_HALO), :]
        hi = win.astype(BF16)
        lo = (win - hi.astype(F32)).astype(BF16)
        u = pad_ref[pl.ds(base, C), :]
        pos = row + c * C
        pooled = jnp.zeros((C, POOL_WIDTH), F32)
        for gi, w in enumerate(POOL_WINDOWS):
            s = (jnp.dot(bands[gi], hi, preferred_element_type=F32)
                 + jnp.dot(bands[gi], lo, preferred_element_type=F32))
            start = pos - w // 2
            cnt = jnp.minimum(start + w, seq_len) - jnp.maximum(start, 0)
            mean = s / cnt.astype(F32)
            pooled = jnp.where(lane // POOL_GROUP == gi, mean, pooled)
        pooled = pooled - u
        mixed = jnp.dot(pooled.astype(BF16), wmat, preferred_element_type=F32) * scale
        o_ref[pl.ds(pl.multiple_of(out_base + c * C, 8), C), :] = mixed.astype(BF16)

    def ctx_body(c, carry):
        chunk(c, ctx0, 0, n_ctx)
        return carry

    def lat_body(c, carry):
        chunk(c, lat0, n_ctx, n_lat)
        return carry

    lax.fori_loop(0, n_ctx // C, ctx_body, 0)
    lax.fori_loop(0, n_lat // C, lat_body, 0)


def _pool(pool_in, w_blk, scale, n_ctx):
    B, S, _ = pool_in.shape
    n_lat = S - n_ctx
    return pl.pallas_call(
        functools.partial(_pool_kernel, n_ctx=n_ctx, n_lat=n_lat),
        out_shape=jax.ShapeDtypeStruct((B, S, POOL_WIDTH), BF16),
        grid=(B,),
        in_specs=[
            pl.BlockSpec((None, S, POOL_WIDTH), lambda b: (b, 0, 0)),
            pl.BlockSpec((POOL_WIDTH, POOL_WIDTH), lambda b: (0, 0)),
            pl.BlockSpec((1, POOL_WIDTH), lambda b: (0, 0)),
        ],
        out_specs=pl.BlockSpec((None, S, POOL_WIDTH), lambda b: (b, 0, 0)),
        scratch_shapes=[pltpu.VMEM((S + 3 * POOL_PAD, POOL_WIDTH), F32)],
        compiler_params=_cparams(("parallel",)),
        name="pool_mixer",
    )(pool_in, w_blk, scale)


def _s5_scan_chunk(u_ref, bcat_ref, lam_ref, ccat_ref, hbuf, state, *, batch, reverse):
    @pl.when(pl.program_id(0) == 0)
    def _():
        state[...] = jnp.zeros(state.shape, F32)

    hbuf[...] = jnp.dot(u_ref[...].astype(BF16), bcat_ref[...], preferred_element_type=F32)
    for cb in range(SSM_LANES // LANE):
        re = slice(cb * LANE, (cb + 1) * LANE)
        im = slice(SSM_LANES + cb * LANE, SSM_LANES + (cb + 1) * LANE)
        lr = jnp.broadcast_to(lam_ref[0:1, re], (batch, LANE))
        li = jnp.broadcast_to(lam_ref[0:1, im], (batch, LANE))
        s_re = state[:, re]
        s_im = state[:, im]
        for step in range(S5_STEPS):
            t = S5_STEPS - 1 - step if reverse else step
            rows = slice(t * batch, (t + 1) * batch)
            n_re = lr * s_re - li * s_im + hbuf[rows, re]
            n_im = lr * s_im + li * s_re + hbuf[rows, im]
            hbuf[rows, re] = n_re
            hbuf[rows, im] = n_im
            s_re, s_im = n_re, n_im
        state[:, re] = s_re
        state[:, im] = s_im
    return jnp.dot(hbuf[...].astype(BF16), ccat_ref[...], preferred_element_type=F32)


def _s5_fwd_kernel(u_ref, bcat_ref, lam_ref, ccat_ref, y_ref, hbuf, state, *, batch):
    y_ref[...] = _s5_scan_chunk(u_ref, bcat_ref, lam_ref, ccat_ref, hbuf, state, batch=batch, reverse=False)


def _s5_bwd_kernel(u_ref, bcat_ref, lam_ref, ccat_ref, yf_ref, dsk_ref, wglu_ref, bglu_ref,
                   o_ref, hbuf, state, *, batch):
    y = _s5_scan_chunk(u_ref, bcat_ref, lam_ref, ccat_ref, hbuf, state, batch=batch, reverse=True)
    tot = yf_ref[...] + y + dsk_ref[...] * u_ref[...]
    g = _gelu_tanh(tot)
    gate = jnp.dot(g.astype(BF16), wglu_ref[...], preferred_element_type=F32) + bglu_ref[...]
    o_ref[...] = (g * _sigmoid(gate)).astype(BF16)


def _s5(us_tm, bcat, lam, ccat, dskip, w_glu, b_glu, batch, n_ctx):
    rows_total = us_tm.shape[0]
    R = S5_STEPS * batch
    nch = rows_total // R
    nctx = n_ctx // S5_STEPS
    const = lambda shape: pl.BlockSpec(shape, lambda j: (0,) * len(shape))
    per_dir = lambda shape: pl.BlockSpec((None,) + shape, None)

    def dir_spec(d, shape):
        return pl.BlockSpec((None,) + shape, lambda j: (d,) + (0,) * len(shape))

    scratch = [pltpu.VMEM((R, 2 * SSM_LANES), F32), pltpu.VMEM((batch, 2 * SSM_LANES), F32)]
    yf = pl.pallas_call(
        functools.partial(_s5_fwd_kernel, batch=batch),
        out_shape=jax.ShapeDtypeStruct((rows_total, SSM_WIDTH), F32),
        grid=(nch,),
        in_specs=[
            pl.BlockSpec((R, SSM_WIDTH), lambda j: (j, 0)),
            dir_spec(0, (SSM_WIDTH, 2 * SSM_LANES)),
            dir_spec(0, (1, 2 * SSM_LANES)),
            dir_spec(0, (2 * SSM_LANES, SSM_WIDTH)),
        ],
        out_specs=pl.BlockSpec((R, SSM_WIDTH), lambda j: (j, 0)),
        scratch_shapes=scratch,
        compiler_params=_cparams(("arbitrary",)),
        name="s5_forward",
    )(us_tm, bcat, lam, ccat)

    def rev(j):
        return (jnp.where(j < nctx, nctx - 1 - j, nch + nctx - 1 - j), 0)

    return pl.pallas_call(
        functools.partial(_s5_bwd_kernel, batch=batch),
        out_shape=jax.ShapeDtypeStruct((rows_total, SSM_WIDTH), BF16),
        grid=(nch,),
        in_specs=[
            pl.BlockSpec((R, SSM_WIDTH), rev),
            dir_spec(1, (SSM_WIDTH, 2 * SSM_LANES)),
            dir_spec(1, (1, 2 * SSM_LANES)),
            dir_spec(1, (2 * SSM_LANES, SSM_WIDTH)),
            pl.BlockSpec((R, SSM_WIDTH), rev),
            const((1, SSM_WIDTH)),
            const((SSM_WIDTH, SSM_WIDTH)),
            const((1, SSM_WIDTH)),
        ],
        out_specs=pl.BlockSpec((R, SSM_WIDTH), rev),
        scratch_shapes=scratch,
        compiler_params=_cparams(("arbitrary",)),
        name="s5_backward",
    )(us_tm, bcat, lam, ccat, yf, dskip, w_glu, b_glu)


def _attn_kernel(q_ref, kc_ref, vc_ref, kp_ref, kq_ref, kn_ref, vp_ref, vq_ref, vn_ref, sink_ref,
                 o_ref, s_ref, e_ref, *, nct, n_lat, tile0):
    i = pl.program_id(1) + tile0
    low = lax.broadcasted_iota(jnp.int32, (BLOCK, LANE), 1) < HEAD_DIM

    def run(kcat, vcat, biases):
        low_v = lax.broadcasted_iota(jnp.int32, vcat.shape, 1) < HEAD_DIM
        one = jnp.ones_like(vcat)
        v_lo = jnp.where(low_v, vcat, one)
        v_hi = jnp.where(low_v, one, vcat)
        npair = ATTN_WIDTH // LANE
        qs = [q_ref[:, p * LANE:(p + 1) * LANE] for p in range(npair)]
        zero = jnp.zeros_like(qs[0])
        q_lo = jnp.concatenate([jnp.where(low, qp, zero) for qp in qs], axis=0)
        q_hi = jnp.concatenate([jnp.where(low, zero, qp) for qp in qs], axis=0)
        col = lambda h0: jnp.concatenate(
            [jnp.broadcast_to(sink_ref[h0 + p:h0 + p + 1, 0:1], (BLOCK, 1)) for p in range(npair)], axis=0)
        sink_lo, sink_hi = col(0), col(Q_PER_KV)
        stacked = [None if b is None else jnp.concatenate([b] * npair, axis=0) for b in biases]
        nk = kcat.shape[1]
        s_ref[0, :, :nk] = jnp.dot(q_lo, kcat, preferred_element_type=F32)
        s_ref[1, :, :nk] = jnp.dot(q_hi, kcat, preferred_element_type=F32)
        ms = []
        for g, sink in enumerate((sink_lo, sink_hi)):
            tiles = []
            for t, bias in enumerate(stacked):
                st = s_ref[g, :, t * LANE:(t + 1) * LANE]
                tiles.append(st if bias is None else st + bias)
            mx = functools.reduce(jnp.maximum, tiles)
            m = jnp.maximum(jnp.max(mx, axis=-1, keepdims=True), sink)
            for t, st in enumerate(tiles):
                e_ref[g, :, t * LANE:(t + 1) * LANE] = jnp.exp(st - m).astype(BF16)
            ms.append(m)
        m_lo, m_hi = ms
        a_lo = jnp.dot(e_ref[0, :, :nk], v_lo, preferred_element_type=F32)
        a_hi = jnp.dot(e_ref[1, :, :nk], v_hi, preferred_element_type=F32)
        low4 = jnp.concatenate([low] * npair, axis=0)
        num = jnp.where(low4, a_lo, a_hi)
        den = (jnp.where(low4, pltpu.roll(a_lo, HEAD_DIM, axis=1), pltpu.roll(a_hi, HEAD_DIM, axis=1))
               + jnp.where(low4, jnp.exp(sink_lo - m_lo), jnp.exp(sink_hi - m_hi)))
        out = (num / den).astype(BF16)
        for p in range(npair):
            o_ref[:, p * LANE:(p + 1) * LANE] = out[p * BLOCK:(p + 1) * BLOCK, :]

    @pl.when(i < nct)
    def _():
        run(kc_ref[...], vc_ref[...], [None] * (kc_ref.shape[1] // LANE))

    @pl.when(i >= nct)
    def _():
        li = i - nct
        r = lax.broadcasted_iota(jnp.int32, (BLOCK, BLOCK), 0)
        c = lax.broadcasted_iota(jnp.int32, (BLOCK, BLOCK), 1)
        bias_p = jnp.where((c >= r) & (li >= 1), 0.0, NEG_INF)
        bias_n = jnp.where((c <= r) & ((li + 2) * BLOCK <= n_lat), 0.0, NEG_INF)
        kcat = jnp.concatenate([kc_ref[...], kp_ref[...], kq_ref[...], kn_ref[...]], axis=1)
        vcat = jnp.concatenate([vc_ref[...], vp_ref[...], vq_ref[...], vn_ref[...]], axis=0)
        run(kcat, vcat, [None] * (kc_ref.shape[1] // LANE) + [bias_p, None, bias_n])


def _attention(q, kt, v, sink_b, n_ctx, lat_only):
    B, S, _ = q.shape
    nct = n_ctx // BLOCK
    ntiles = S // BLOCK
    tile0 = nct if lat_only else 0
    rows_out = S - tile0 * BLOCK
    cur = lambda i: i + tile0
    prev = lambda i: jnp.maximum(i + tile0 - 1, nct)
    nxt = lambda i: jnp.minimum(i + tile0 + 1, ntiles - 1)
    vblk = lambda f: pl.BlockSpec((None, BLOCK, KV_WIDTH), lambda b, i: (b, f(i), 0))
    kblk = lambda f: pl.BlockSpec((None, KV_WIDTH, BLOCK), lambda b, i: (b, 0, f(i)))
    return pl.pallas_call(
        functools.partial(_attn_kernel, nct=nct, n_lat=S - n_ctx, tile0=tile0),
        out_shape=jax.ShapeDtypeStruct((B, rows_out, ATTN_WIDTH), BF16),
        grid=(B, ntiles - tile0),
        in_specs=[
            pl.BlockSpec((None, BLOCK, ATTN_WIDTH), lambda b, i: (b, cur(i), 0)),
            pl.BlockSpec((None, KV_WIDTH, n_ctx), lambda b, i: (b, 0, 0)),
            pl.BlockSpec((None, n_ctx, KV_WIDTH), lambda b, i: (b, 0, 0)),
            kblk(prev), kblk(cur), kblk(nxt),
            vblk(prev), vblk(cur), vblk(nxt),
            pl.BlockSpec((N_Q_HEADS, LANE), lambda b, i: (0, 0)),
        ],
        out_specs=pl.BlockSpec((None, BLOCK, ATTN_WIDTH), lambda b, i: (b, i, 0)),
        scratch_shapes=[pltpu.VMEM((N_KV_HEADS, Q_PER_KV * BLOCK, n_ctx + 3 * BLOCK), F32),
                        pltpu.VMEM((N_KV_HEADS, Q_PER_KV * BLOCK, n_ctx + 3 * BLOCK), BF16)],
        compiler_params=_cparams(("parallel", "parallel")),
        name="attention",
    )(q, kt, v, kt, kt, kt, v, v, v, sink_b)


def _out_proj_kernel(a_ref, b_ref, c_ref, x_ref, mod_ref, wa_ref, wb_ref, wc_ref, gpost_ref, gpre_ref,
                     x1_ref, h2_ref, tmp_ref):
    y = (jnp.dot(a_ref[...], wa_ref[...], preferred_element_type=F32)
         + jnp.dot(b_ref[...], wb_ref[...], preferred_element_type=F32)
         + jnp.dot(c_ref[...], wc_ref[...], preferred_element_type=F32))
    x1 = x_ref[...] + mod_ref[2:3, :] * _rms(y, gpost_ref[...])
    x1_ref[...] = x1
    h2 = _rms(x1, gpre_ref[...]) * (1.0 + mod_ref[4:5, :]) + mod_ref[3:4, :]
    nl = tmp_ref.shape[0]
    T = h2.shape[0]
    G = T // SUBLANES
    pitch = G + SUBLANES
    for c in range(nl):
        for s in range(SUBLANES):
            tmp_ref[c, s * pitch:s * pitch + G, :] = h2[s * G:(s + 1) * G, c * LANE:(c + 1) * LANE]
    pack = 2
    for v in range(0, G, pack):
        blk = jnp.concatenate(
            [jnp.concatenate([tmp_ref[c, pl.ds(v + d, SUBLANES, stride=pitch), :] for c in range(nl)], axis=1)
             for d in range(pack)], axis=0)
        h2_ref[v * SUBLANES:(v + pack) * SUBLANES, :] = blk.astype(BF16)


def _out_proj(a, bs, att, xc, mod, wa, wb, wc, gpost, gpre, n_ctx, lat_only):
    B, S, _ = xc.shape
    T = TOK_TILE
    nct = n_ctx // T
    tile0 = nct if lat_only else 0
    rows_out = S - tile0 * T
    full = lambda width: pl.BlockSpec((None, T, width), lambda b, i: (b, i + tile0, 0))
    outs = lambda width: pl.BlockSpec((None, T, width), lambda b, i: (b, i, 0))
    const = lambda shape: pl.BlockSpec(shape, lambda b, i: (0,) * len(shape))
    return pl.pallas_call(
        _out_proj_kernel,
        out_shape=(jax.ShapeDtypeStruct((B, rows_out, D_MODEL), F32),
                   jax.ShapeDtypeStruct((B, rows_out, D_MODEL), BF16)),
        grid=(B, S // T - tile0),
        in_specs=[
            full(POOL_WIDTH),
            pl.BlockSpec((T, SSM_WIDTH), lambda b, i: (i + tile0, b)),
            outs(ATTN_WIDTH),
            full(D_MODEL),
            pl.BlockSpec((None, None, 6, D_MODEL), lambda b, i: (b, jnp.where(i + tile0 >= nct, 1, 0), 0, 0)),
            const((POOL_WIDTH, D_MODEL)), const((SSM_WIDTH, D_MODEL)), const((ATTN_WIDTH, D_MODEL)),
            const((1, D_MODEL)), const((1, D_MODEL)),
        ],
        out_specs=(outs(D_MODEL), outs(D_MODEL)),
        scratch_shapes=[pltpu.VMEM((D_MODEL // LANE, T + SUBLANES * SUBLANES, LANE), F32)],
        compiler_params=_cparams(("parallel", "parallel")),
        name="out_proj",
    )(a, bs, att, xc, mod, wa, wb, wc, gpost, gpre)


def _ffn_kernel(h_ref, hp_ref, hn_ref, x_ref, mod_ref, wu_ref, cw_ref, cb_ref, wd_ref, g_ref,
                o_ref, act_ref, y_ref, *, seq_starts, seq_ends):
    i = pl.program_id(1)
    T = h_ref.shape[0]
    G = T // SUBLANES
    first = functools.reduce(jnp.logical_or, [i == s for s in seq_starts])
    last = functools.reduce(jnp.logical_or, [i == e for e in seq_ends])
    half = HALO_ROWS // 2
    lhs = jnp.concatenate([h_ref[...], hp_ref[half:, :], hn_ref[:half, :]], axis=0)
    sub = lax.broadcasted_iota(jnp.int32, (SUBLANES, 2 * FF_CHUNK), 0)

    for j in range(wu_ref.shape[0]):
        ue = jnp.dot(lhs, wu_ref[j], preferred_element_type=F32)
        u = ue[:T, :]
        edge = ue[T:, :]
        prev_row = jnp.where(first, 0.0, edge[half - 1:half, :])
        next_row = jnp.where(last, 0.0, edge[half:half + 1, :])
        head = jnp.where(sub == 0, prev_row, pltpu.roll(u[T - SUBLANES:, :], 1, axis=0))
        tail = jnp.where(sub == SUBLANES - 1, next_row, pltpu.roll(u[:SUBLANES, :], SUBLANES - 1, axis=0))
        up = jnp.concatenate([head, u[:T - SUBLANES, :]], axis=0)
        un = jnp.concatenate([u[SUBLANES:, :], tail], axis=0)
        cw = cw_ref[j]
        c = cw[0:1, :] * up + cw[1:2, :] * u + cw[2:3, :] * un + cb_ref[j]
        val = c[:, :FF_CHUNK]
        gate = c[:, FF_CHUNK:]
        act = gate * _sigmoid(gate) * val
        act_ref[:, j * FF_CHUNK:(j + 1) * FF_CHUNK] = act.astype(BF16)
    y = jnp.dot(act_ref[...], wd_ref[...], preferred_element_type=F32)
    r = _rms(y, g_ref[...])
    nl = y_ref.shape[0]
    for c in range(nl):
        y_ref[c] = r[:, c * LANE:(c + 1) * LANE]
    gt = mod_ref[5:6, :]
    for k in range(G):
        src = pl.ds((SUBLANES * k % G) * SUBLANES + SUBLANES * k // G, SUBLANES, stride=SUBLANES)
        rows = slice(k * SUBLANES, (k + 1) * SUBLANES)
        yk = jnp.concatenate([y_ref[c, src, :] for c in range(nl)], axis=1)
        o_ref[rows, :] = x_ref[rows, :] + gt * yk


def _ffn(h2, x1, mod, wu, cw, cb, wd, g, n_ctx, lat_only):
    B, rows, _ = x1.shape
    T = TOK_TILE
    ntiles = rows // T
    nct = 0 if lat_only else n_ctx // T
    seq_starts = (0,) if lat_only else (0, nct)
    seq_ends = (ntiles - 1,) if lat_only else (nct - 1, ntiles - 1)
    hb = T // HALO_ROWS
    nhb = rows // HALO_ROWS
    nch = wu.shape[0]
    tok = lambda width: pl.BlockSpec((None, T, width), lambda b, i: (b, i, 0))
    const = lambda shape: pl.BlockSpec(shape, lambda b, i: (0,) * len(shape))
    return pl.pallas_call(
        functools.partial(_ffn_kernel, seq_starts=seq_starts, seq_ends=seq_ends),
        out_shape=jax.ShapeDtypeStruct((B, rows, D_MODEL), F32),
        grid=(B, ntiles),
        in_specs=[
            tok(D_MODEL),
            pl.BlockSpec((None, HALO_ROWS, D_MODEL), lambda b, i: (b, jnp.maximum(i * hb - 1, 0), 0)),
            pl.BlockSpec((None, HALO_ROWS, D_MODEL), lambda b, i: (b, jnp.minimum((i + 1) * hb, nhb - 1), 0)),
            tok(D_MODEL),
            pl.BlockSpec((None, None, 6, D_MODEL),
                         lambda b, i: (b, jnp.where(i >= nct, 1, 0), 0, 0)),
            const((nch, D_MODEL, 2 * FF_CHUNK)),
            const((nch, 3, 2 * FF_CHUNK)),
            const((nch, 1, 2 * FF_CHUNK)),
            const((D_FF, D_MODEL)),
            const((1, D_MODEL)),
        ],
        out_specs=tok(D_MODEL),
        scratch_shapes=[pltpu.VMEM((T, D_FF), BF16), pltpu.VMEM((D_MODEL // LANE, T, LANE), F32)],
        compiler_params=_cparams(("parallel", "parallel")),
        name="conv_ffn",
    )(h2, h2, h2, x1, mod, wu, cw, cb, wd, g)


def _zoh_blocks(lam_re, lam_im, log_dt, b_re, b_im, c_re, c_im):
    lr = jnp.minimum(lam_re, -1e-4)
    li = lam_im
    dt = jnp.exp(log_dt)[..., None]
    mag = jnp.exp(lr * dt)
    lbr = mag * jnp.cos(li * dt)
    lbi = mag * jnp.sin(li * dt)
    den = lr * lr + li * li
    fr = ((lbr - 1.0) * lr + lbi * li) / den
    fi = (lbi * lr - (lbr - 1.0) * li) / den
    bbr = fr[..., None] * b_re - fi[..., None] * b_im
    bbi = fr[..., None] * b_im + fi[..., None] * b_re
    eye = jnp.eye(SSM_GROUPS, dtype=F32)
    blk_b = lambda t: jnp.einsum('dgph,gk->dghkp', t, eye).reshape(2, SSM_WIDTH, SSM_LANES)
    blk_c = lambda t: jnp.einsum('dghp,gk->dgpkh', t, eye).reshape(2, SSM_LANES, SSM_WIDTH)
    bcat = jnp.concatenate([blk_b(bbr), blk_b(bbi)], axis=-1).astype(BF16)
    ccat = jnp.concatenate([blk_c(c_re), -blk_c(c_im)], axis=1).astype(BF16)
    lam = jnp.concatenate([lbr.reshape(2, 1, SSM_LANES), lbi.reshape(2, 1, SSM_LANES)], axis=-1)
    return bcat, lam, ccat


def _head_perm():
    cols = []
    for p in range(Q_PER_KV):
        for h in (p, Q_PER_KV + p):
            cols.extend(range(h * HEAD_DIM, (h + 1) * HEAD_DIM))
    return jnp.array(cols, dtype=jnp.int32)


def _rope_tables(n_ctx, n_lat):
    t = jnp.arange(n_lat)
    lane = jnp.arange(LANE)
    hl = lane % HEAD_DIM
    quarter = HEAD_DIM // 4
    inv_freq = ROPE_BASE ** (-jnp.arange(quarter, dtype=F32) / quarter)
    pos = jnp.where(hl[None, :] < HEAD_DIM // 2, (t // GRID_W)[:, None], (t % GRID_W)[:, None])
    ang = pos.astype(F32) * inv_freq[hl % quarter][None, :]
    first_half = (hl % (2 * quarter)) < quarter
    cos_l = jnp.cos(ang)
    sin_l = jnp.where(first_half[None, :], -jnp.sin(ang), jnp.sin(ang))
    cos_t = jnp.concatenate([jnp.ones((n_ctx, LANE), F32), cos_l], axis=0)
    sin_t = jnp.concatenate([jnp.zeros((n_ctx, LANE), F32), sin_l], axis=0)
    return cos_t, sin_t


def kernel(x, c, ctx, c_ctx, w_ada, b_ada, g_pre_mix, g_post_mix, g_pre_ffn, g_post_ffn, w_in, pool_w, pool_scale, lam_re, lam_im, log_dt, b_re, b_im, c_re, c_im, s5_d, w_glu, b_glu, sink, w_out, w_up, conv_w, conv_b, w_down):
    B, L, _ = x.shape
    N = ctx.shape[1]
    S = N + L
    assert N % TOK_TILE == 0 and L % TOK_TILE == 0 and L % GRID_W == 0

    rows = -(-(B + 1) // 8) * 8
    cond = jnp.concatenate([c, c_ctx[None, :], jnp.zeros((rows - B - 1, D_MODEL), F32)], axis=0)
    mod_all = _ada(cond, w_ada, b_ada)
    mod_lat = mod_all[:, :B].reshape(DEPTH, B, 1, 6, D_MODEL)
    mod_ctx = jnp.broadcast_to(mod_all[:, B].reshape(DEPTH, 1, 1, 6, D_MODEL), (DEPTH, B, 1, 6, D_MODEL))
    mods = jnp.concatenate([mod_ctx, mod_lat], axis=2)

    cos_t, sin_t = _rope_tables(N, L)
    perm = _head_perm()
    q0 = POOL_WIDTH + SSM_WIDTH
    in_cols = jnp.concatenate([jnp.arange(q0), q0 + perm, jnp.arange(q0 + ATTN_WIDTH, D_IN)])
    out_rows = jnp.concatenate([jnp.arange(q0), q0 + perm])
    eye_w = jnp.eye(len(POOL_WINDOWS), dtype=F32)
    nch = D_FF // FF_CHUNK

    def ffn_chunks(t):
        r = t.shape[0]
        return t.reshape(r, 2, nch, FF_CHUNK).transpose(2, 0, 1, 3).reshape(nch, r, 2 * FF_CHUNK)

    xc = jnp.concatenate([ctx, x], axis=1)
    for l in range(DEPTH):
        last = l == DEPTH - 1
        w_in_l = w_in[l][:, in_cols].astype(BF16)
        w_out_l = w_out[l][out_rows].astype(BF16)
        pool_blk = jnp.einsum('gcd,gk->gckd', pool_w[l], eye_w).reshape(POOL_WIDTH, POOL_WIDTH).astype(BF16)
        bcat, lam, ccat = _zoh_blocks(lam_re[l], lam_im[l], log_dt[l], b_re[l], b_im[l], c_re[l], c_im[l])
        sink_b = jnp.broadcast_to(sink[l][:, None], (N_Q_HEADS, LANE))
        wu = ffn_chunks(w_up[l]).astype(BF16)
        cw = ffn_chunks(conv_w[l])
        cb = ffn_chunks(conv_b[l][None, :])
        wd = w_down[l].astype(BF16)

        pool_in, us, q, kt, v = _in_proj(xc, mods[l], g_pre_mix[l][None, :], w_in_l, cos_t, sin_t, N // TOK_TILE)
        a = _pool(pool_in, pool_blk, pool_scale[l][None, :], N)
        bs = _s5(us.reshape(S * B, SSM_WIDTH), bcat, lam, ccat, s5_d[l][None, :],
                 w_glu[l].astype(BF16), b_glu[l][None, :], B, N)
        att = _attention(q, kt, v, sink_b, N, last)
        x1, h2 = _out_proj(a, bs.reshape(S, B * SSM_WIDTH), att, xc, mods[l],
                           w_out_l[:POOL_WIDTH], w_out_l[POOL_WIDTH:q0], w_out_l[q0:],
                           g_post_mix[l][None, :], g_pre_ffn[l][None, :], N, last)
        xc = _ffn(h2, x1, mods[l], wu, cw, cb, wd, g_post_ffn[l][None, :], N, last)
    return xc
```

```python
import functools
import math

import jax
import jax.numpy as jnp
from jax import lax
from jax.experimental import pallas as pl
from jax.experimental.pallas import tpu as pltpu

F32 = jnp.float32
BF16 = jnp.bfloat16

D_MODEL = 1024
DEPTH = 4
GRID_W = 64
EPS = 1e-6
POOL_WINDOWS = (2, 4, 8, 16)
POOL_WIDTH = 256
POOL_GROUP = 64
SSM_WIDTH = 256
SSM_CH = 16
SSM_GROUPS = 16
SSM_STATE = 64
SSM_LANES = SSM_GROUPS * SSM_STATE
HEAD_DIM = 64
ATTN_WIDTH = 512
N_Q_HEADS = 8
N_KV_HEADS = 2
Q_PER_KV = 4
KV_WIDTH = 128
WINDOW = 128
BLOCK = 128
ROPE_BASE = 10000.0
D_FF = 2816
D_IN = 1280
NEG_INF = -1e30
LOG2_E = 1.4426950408889634

LANE = 128
POOL_PAD = 16
POOL_HALO = 8
POOL_CHUNK = 128
TOK_TILE = 256
S5_STEPS = 16
FF_CHUNK = 256
HALO_ROWS = 16
SUBLANES = 8
IN_SPLIT = 2
ATTN_UNIT_PAIRS = ((0, 1, 2, 3),)
VMEM_LIMIT = 56 * 1024 * 1024


def _cparams(sem):
    return pltpu.CompilerParams(dimension_semantics=sem, vmem_limit_bytes=VMEM_LIMIT)


def _rms(x, g):
    return x * lax.rsqrt(jnp.mean(x * x, axis=-1, keepdims=True) + EPS) * g


def _sigmoid(x):
    return 1.0 / (1.0 + jnp.exp(-x))


def _gelu_tanh(x):
    c = math.sqrt(2.0 / math.pi)
    return 0.5 * x * (1.0 + jnp.tanh(c * (x + 0.044715 * (x * x * x))))


def _ada_kernel(c_ref, w_ref, b_ref, o_ref):
    c = c_ref[...]
    s = c * _sigmoid(c)
    o_ref[...] = jnp.dot(s.astype(BF16), w_ref[...].astype(BF16), preferred_element_type=F32) + b_ref[...]


def _ada(cond, w_ada, b_ada):
    rows = cond.shape[0]
    nblk = w_ada.shape[-1] // D_MODEL
    return pl.pallas_call(
        _ada_kernel,
        out_shape=jax.ShapeDtypeStruct((DEPTH, rows, nblk * D_MODEL), F32),
        grid=(DEPTH, nblk),
        in_specs=[
            pl.BlockSpec((rows, D_MODEL), lambda l, j: (0, 0)),
            pl.BlockSpec((None, D_MODEL, D_MODEL), lambda l, j: (l, 0, j)),
            pl.BlockSpec((None, 1, D_MODEL), lambda l, j: (l, 0, j)),
        ],
        out_specs=pl.BlockSpec((None, rows, D_MODEL), lambda l, j: (l, 0, j)),
        compiler_params=_cparams(("arbitrary", "arbitrary")),
        name="ada_mod",
    )(cond, w_ada, b_ada.reshape(DEPTH, 1, -1))


def _in_proj_kernel(x_ref, mod_ref, g_ref, w_ref, cos_ref, sin_ref,
                    pool_ref, us_ref, q_ref, kt_ref, v_ref):
    T = x_ref.shape[0]
    lane = lax.broadcasted_iota(jnp.int32, (T // IN_SPLIT, LANE), 1)
    first_half = (lane % 32) < 16
    q0 = POOL_WIDTH + SSM_WIDTH
    k0 = q0 + ATTN_WIDTH
    scale = HEAD_DIM ** -0.5 * LOG2_E
    w = w_ref[...]
    for part in range(IN_SPLIT):
        rows = slice(part * (T // IN_SPLIT), (part + 1) * (T // IN_SPLIT))
        h = _rms(x_ref[rows, :], g_ref[...]) * (1.0 + mod_ref[1:2, :]) + mod_ref[0:1, :]
        z = jnp.dot(h.astype(BF16), w, preferred_element_type=F32)
        pool_ref[rows, :] = z[:, 0:POOL_WIDTH]
        us_ref[rows, :] = z[:, POOL_WIDTH:POOL_WIDTH + SSM_WIDTH]
        cos = cos_ref[rows, :]
        sin = sin_ref[rows, :]

        def rope(t):
            partner = jnp.where(first_half, pltpu.roll(t, LANE - 16, axis=1), pltpu.roll(t, 16, axis=1))
            return t * cos + partner * sin

        for j in range(ATTN_WIDTH // LANE):
            t = z[:, q0 + j * LANE:q0 + (j + 1) * LANE]
            q_ref[rows, j * LANE:(j + 1) * LANE] = (rope(t) * scale).astype(BF16)
        kt_ref[:, rows] = rope(z[:, k0:k0 + KV_WIDTH]).T.astype(BF16)
        v_ref[rows, :] = z[:, k0 + KV_WIDTH:k0 + 2 * KV_WIDTH].astype(BF16)


def _in_proj(xc, mod, g, w_in, cos_t, sin_t, nct):
    B, S, _ = xc.shape
    T = TOK_TILE
    grid = (B, S // T)
    tok = lambda width: pl.BlockSpec((None, T, width), lambda b, i: (b, i, 0))
    return pl.pallas_call(
        _in_proj_kernel,
        out_shape=(
            jax.ShapeDtypeStruct((B, S, POOL_WIDTH), F32),
            jax.ShapeDtypeStruct((S, B * SSM_WIDTH), F32),
            jax.ShapeDtypeStruct((B, S, ATTN_WIDTH), BF16),
            jax.ShapeDtypeStruct((B, KV_WIDTH, S), BF16),
            jax.ShapeDtypeStruct((B, S, KV_WIDTH), BF16),
        ),
        grid=grid,
        in_specs=[
            tok(D_MODEL),
            pl.BlockSpec((None, None, 6, D_MODEL), lambda b, i: (b, jnp.where(i >= nct, 1, 0), 0, 0)),
            pl.BlockSpec((1, D_MODEL), lambda b, i: (0, 0)),
            pl.BlockSpec((D_MODEL, D_IN), lambda b, i: (0, 0)),
            pl.BlockSpec((T, LANE), lambda b, i: (i, 0)),
            pl.BlockSpec((T, LANE), lambda b, i: (i, 0)),
        ],
        out_specs=(
            tok(POOL_WIDTH),
            pl.BlockSpec((T, SSM_WIDTH), lambda b, i: (i, b)),
            tok(ATTN_WIDTH),
            pl.BlockSpec((None, KV_WIDTH, T), lambda b, i: (b, 0, i)),
            tok(KV_WIDTH),
        ),
        compiler_params=_cparams(("parallel", "parallel")),
        name="in_proj",
    )(xc, mod, g, w_in, cos_t, sin_t)


def _pool_kernel(u_ref, w_ref, sc_ref, o_ref, pad_ref, *, n_ctx, n_lat):
    C = POOL_CHUNK
    zeros = jnp.zeros((POOL_PAD, POOL_WIDTH), F32)
    ctx0 = POOL_PAD
    lat0 = 2 * POOL_PAD + n_ctx
    pad_ref[0:POOL_PAD, :] = zeros
    pad_ref[ctx0 + n_ctx:lat0, :] = zeros
    pad_ref[lat0 + n_lat:lat0 + n_lat + POOL_PAD, :] = zeros
    pad_ref[ctx0:ctx0 + n_ctx, :] = u_ref[0:n_ctx, :]
    pad_ref[lat0:lat0 + n_lat, :] = u_ref[n_ctx:n_ctx + n_lat, :]

    nw = len(POOL_WINDOWS)
    t_idx = lax.broadcasted_iota(jnp.int32, (C, C + 2 * POOL_HALO), 0)
    s_idx = lax.broadcasted_iota(jnp.int32, (C, C + 2 * POOL_HALO), 1) - POOL_HALO
    band = jnp.concatenate(
        [jnp.where((s_idx >= t_idx - w // 2) & (s_idx < t_idx - w // 2 + w), 1.0, 0.0).astype(BF16)
         for w in POOL_WINDOWS], axis=0)
    grp = lax.broadcasted_iota(jnp.int32, (C, POOL_WIDTH), 1) // POOL_GROUP
    row = lax.broadcasted_iota(jnp.int32, (C, POOL_WIDTH), 0)
    w_lane = functools.reduce(lambda acc, gw: jnp.where(grp == gw[0], gw[1], acc),
                              list(enumerate(POOL_WINDOWS))[1:], jnp.full((C, POOL_WIDTH), POOL_WINDOWS[0]))
    inv_w = 1.0 / w_lane.astype(F32)
    wmat = w_ref[...]
    scale = sc_ref[...]

    def window_sums(c, pad_base):
        base = pad_base + c * C
        win = pad_ref[base - POOL_HALO:base + C + POOL_HALO, :]
        hi = win.astype(BF16)
        lo = (win - hi.astype(F32)).astype(BF16)
        return (jnp.dot(band, hi, preferred_element_type=F32)
                + jnp.dot(band, lo, preferred_element_type=F32))

    def finish(sums, c, pad_base, out_base, seq_len):
        base = pad_base + c * C
        pooled = sums[0:C, :]
        for gi in range(1, nw):
            pooled = jnp.where(grp == gi, sums[gi * C:(gi + 1) * C, :], pooled)
        if c * C >= POOL_HALO and (c + 1) * C + POOL_HALO <= seq_len:
            pooled = pooled * inv_w
        else:
            start = row + c * C - w_lane // 2
            cnt = jnp.minimum(start + w_lane, seq_len) - jnp.maximum(start, 0)
            pooled = pooled / cnt.astype(F32)
        pooled = pooled - pad_ref[base:base + C, :]
        mixed = jnp.dot(pooled.astype(BF16), wmat, preferred_element_type=F32) * scale
        o_ref[out_base + c * C:out_base + (c + 1) * C, :] = mixed.astype(BF16)

    jobs = ([(c, ctx0, 0, n_ctx) for c in range(n_ctx // C)]
            + [(c, lat0, n_ctx, n_lat) for c in range(n_lat // C)])
    sums = window_sums(jobs[0][0], jobs[0][1])
    for k, job in enumerate(jobs):
        nxt = window_sums(jobs[k + 1][0], jobs[k + 1][1]) if k + 1 < len(jobs) else None
        finish(sums, *job)
        sums = nxt


def _pool(pool_in, w_blk, scale, n_ctx):
    B, S, _ = pool_in.shape
    n_lat = S - n_ctx
    return pl.pallas_call(
        functools.partial(_pool_kernel, n_ctx=n_ctx, n_lat=n_lat),
        out_shape=jax.ShapeDtypeStruct((B, S, POOL_WIDTH), BF16),
        grid=(B,),
        in_specs=[
            pl.BlockSpec((None, S, POOL_WIDTH), lambda b: (b, 0, 0)),
            pl.BlockSpec((POOL_WIDTH, POOL_WIDTH), lambda b: (0, 0)),
            pl.BlockSpec((1, POOL_WIDTH), lambda b: (0, 0)),
        ],
        out_specs=pl.BlockSpec((None, S, POOL_WIDTH), lambda b: (b, 0, 0)),
        scratch_shapes=[pltpu.VMEM((S + 3 * POOL_PAD, POOL_WIDTH), F32)],
        compiler_params=_cparams(("parallel",)),
        name="pool_mixer",
    )(pool_in, w_blk, scale)


def _s5_scan(lam_ref, bu_ref, hb_ref, state, *, batch, reverse):
    for cb in range(SSM_LANES // LANE):
        re = slice(cb * LANE, (cb + 1) * LANE)
        im = slice(SSM_LANES + cb * LANE, SSM_LANES + (cb + 1) * LANE)
        lr = jnp.broadcast_to(lam_ref[0:1, re], (batch, LANE))
        li = jnp.broadcast_to(lam_ref[0:1, im], (batch, LANE))
        s_re = state[:, re]
        s_im = state[:, im]
        for step in range(S5_STEPS):
            t = S5_STEPS - 1 - step if reverse else step
            rows = slice(t * batch, (t + 1) * batch)
            n_re = lr * s_re - li * s_im + bu_ref[rows, re]
            n_im = lr * s_im + li * s_re + bu_ref[rows, im]
            hb_ref[rows, re] = n_re.astype(BF16)
            hb_ref[rows, im] = n_im.astype(BF16)
            s_re, s_im = n_re, n_im
        state[:, re] = s_re
        state[:, im] = s_im


def _s5_pair(u_ref, bcat_ref, lam_ref, ccat_ref, bu0, bu1, hb0, hb1, state, *, batch, reverse):
    R = S5_STEPS * batch

    @pl.when(pl.program_id(0) == 0)
    def _():
        state[...] = jnp.zeros(state.shape, F32)
        for ref in (bu0, bu1, hb0, hb1):
            ref[...] = jnp.zeros(ref.shape, ref.dtype)

    first, second = (slice(R, 2 * R), slice(0, R)) if reverse else (slice(0, R), slice(R, 2 * R))
    bcat = bcat_ref[...]
    ccat = ccat_ref[...]
    y_a = jnp.dot(hb0[...], ccat, preferred_element_type=F32)
    bu0[...] = jnp.dot(u_ref[first, :].astype(BF16), bcat, preferred_element_type=F32)
    _s5_scan(lam_ref, bu1, hb1, state, batch=batch, reverse=reverse)
    bu1[...] = jnp.dot(u_ref[second, :].astype(BF16), bcat, preferred_element_type=F32)
    y_b = jnp.dot(hb1[...], ccat, preferred_element_type=F32)
    _s5_scan(lam_ref, bu0, hb0, state, batch=batch, reverse=reverse)
    return jnp.concatenate([y_b, y_a] if reverse else [y_a, y_b], axis=0)


def _s5_fwd_kernel(u_ref, bcat_ref, lam_ref, ccat_ref, y_ref, bu0, bu1, hb0, hb1, state, *, batch):
    y_ref[...] = _s5_pair(u_ref, bcat_ref, lam_ref, ccat_ref, bu0, bu1, hb0, hb1, state,
                          batch=batch, reverse=False)


def _s5_bwd_kernel(u_ref, bcat_ref, lam_ref, ccat_ref, u2_ref, yf_ref, dsk_ref, wglu_ref, bglu_ref,
                   o_ref, bu0, bu1, hb0, hb1, state, *, batch):
    y = _s5_pair(u_ref, bcat_ref, lam_ref, ccat_ref, bu0, bu1, hb0, hb1, state, batch=batch, reverse=True)
    tot = yf_ref[...] + y + dsk_ref[...] * u2_ref[...]
    g = _gelu_tanh(tot)
    gate = jnp.dot(g.astype(BF16), wglu_ref[...], preferred_element_type=F32) + bglu_ref[...]
    o_ref[...] = (g * _sigmoid(gate)).astype(BF16)


def _s5(us_tm, bcat, lam, ccat, dskip, w_glu, b_glu, batch, n_ctx):
    rows_total = us_tm.shape[0]
    R = S5_STEPS * batch
    nch = rows_total // (2 * R)
    nctx = n_ctx // (2 * S5_STEPS)
    assert rows_total % (2 * R) == 0 and n_ctx % (2 * S5_STEPS) == 0
    const = lambda shape: pl.BlockSpec(shape, lambda j: (0,) * len(shape))

    def dir_spec(d, shape):
        return pl.BlockSpec((None,) + shape, lambda j: (d,) + (0,) * len(shape))

    def chunk_spec(order, lag):
        return pl.BlockSpec((2 * R, SSM_WIDTH), lambda j: (order(jnp.clip(j - lag, 0, nch - 1)), 0))

    fwd = lambda c: c
    rev = lambda c: jnp.where(c < nctx, nctx - 1 - c, nch + nctx - 1 - c)

    scratch = [pltpu.VMEM((R, 2 * SSM_LANES), F32), pltpu.VMEM((R, 2 * SSM_LANES), F32),
               pltpu.VMEM((R, 2 * SSM_LANES), BF16), pltpu.VMEM((R, 2 * SSM_LANES), BF16),
               pltpu.VMEM((batch, 2 * SSM_LANES), F32)]
    yf = pl.pallas_call(
        functools.partial(_s5_fwd_kernel, batch=batch),
        out_shape=jax.ShapeDtypeStruct((rows_total, SSM_WIDTH), F32),
        grid=(nch + 1,),
        in_specs=[
            chunk_spec(fwd, 0),
            dir_spec(0, (SSM_WIDTH, 2 * SSM_LANES)),
            dir_spec(0, (1, 2 * SSM_LANES)),
            dir_spec(0, (2 * SSM_LANES, SSM_WIDTH)),
        ],
        out_specs=chunk_spec(fwd, 1),
        scratch_shapes=scratch,
        compiler_params=_cparams(("arbitrary",)),
        name="s5_forward",
    )(us_tm, bcat, lam, ccat)

    return pl.pallas_call(
        functools.partial(_s5_bwd_kernel, batch=batch),
        out_shape=jax.ShapeDtypeStruct((rows_total, SSM_WIDTH), BF16),
        grid=(nch + 1,),
        in_specs=[
            chunk_spec(rev, 0),
            dir_spec(1, (SSM_WIDTH, 2 * SSM_LANES)),
            dir_spec(1, (1, 2 * SSM_LANES)),
            dir_spec(1, (2 * SSM_LANES, SSM_WIDTH)),
            chunk_spec(rev, 1),
            chunk_spec(rev, 1),
            const((1, SSM_WIDTH)),
            const((SSM_WIDTH, SSM_WIDTH)),
            const((1, SSM_WIDTH)),
        ],
        out_specs=chunk_spec(rev, 1),
        scratch_shapes=scratch,
        compiler_params=_cparams(("arbitrary",)),
        name="s5_backward",
    )(us_tm, bcat, lam, ccat, us_tm, yf, dskip, w_glu, b_glu)


def _attn_kernel(q_ref, kc_ref, vc_ref, kp_ref, kq_ref, kn_ref, vp_ref, vq_ref, vn_ref, sink_ref,
                 o_ref, s_ref, e_ref, *, nct, n_lat, tile0):
    i = pl.program_id(1) + tile0
    low = lax.broadcasted_iota(jnp.int32, (BLOCK, LANE), 1) < HEAD_DIM

    def run(kcat, vcat, biases):
        low_v = lax.broadcasted_iota(jnp.int32, vcat.shape, 1) < HEAD_DIM
        one = jnp.ones_like(vcat)
        vones = (jnp.where(low_v, vcat, one), jnp.where(low_v, one, vcat))
        npair = ATTN_WIDTH // LANE
        nk = kcat.shape[1]
        qs = [q_ref[:, p * LANE:(p + 1) * LANE] for p in range(npair)]
        zero = jnp.zeros_like(qs[0])
        units = [(half, pairs) for pairs in ATTN_UNIT_PAIRS for half in (0, 1)]
        per_unit = len(ATTN_UNIT_PAIRS[0])
        stacked = [None if b is None else jnp.concatenate([b] * per_unit, axis=0) for b in biases]

        def scores(u):
            half, pairs = units[u]
            q = jnp.concatenate([jnp.where(low, qs[p], zero) if half == 0 else jnp.where(low, zero, qs[p])
                                 for p in pairs], axis=0)
            s_ref[u, :, :nk] = jnp.dot(q, kcat, preferred_element_type=F32)

        def softmax(u):
            half, pairs = units[u]
            sink = jnp.concatenate(
                [jnp.broadcast_to(sink_ref[half * Q_PER_KV + p:half * Q_PER_KV + p + 1, 0:1], (BLOCK, 1))
                 for p in pairs], axis=0)
            tiles = []
            for t, bias in enumerate(stacked):
                st = s_ref[u, :, t * LANE:(t + 1) * LANE]
                tiles.append(st if bias is None else st + bias)
            mx = functools.reduce(jnp.maximum, tiles)
            m = jnp.maximum(jnp.max(mx, axis=-1, keepdims=True), sink)
            for t, st in enumerate(tiles):
                e_ref[u, :, t * LANE:(t + 1) * LANE] = jnp.exp2(st - m).astype(BF16)
            return jnp.exp2(sink - m)

        def values(u):
            return jnp.dot(e_ref[u, :, :nk], vones[units[u][0]], preferred_element_type=F32)

        nu = len(units)
        sink_terms, accs = [None] * nu, [None] * nu
        scores(0)
        scores(1)
        for u in range(nu):
            sink_terms[u] = softmax(u)
            if u + 2 < nu:
                scores(u + 2)
            accs[u] = values(u)
        low2 = jnp.concatenate([low] * per_unit, axis=0)
        for g in range(nu // 2):
            a_lo, a_hi = accs[2 * g], accs[2 * g + 1]
            num = jnp.where(low2, a_lo, a_hi)
            den = (jnp.where(low2, pltpu.roll(a_lo, HEAD_DIM, axis=1), pltpu.roll(a_hi, HEAD_DIM, axis=1))
                   + jnp.where(low2, sink_terms[2 * g], sink_terms[2 * g + 1]))
            out = (num / den).astype(BF16)
            for r, p in enumerate(units[2 * g][1]):
                o_ref[:, p * LANE:(p + 1) * LANE] = out[r * BLOCK:(r + 1) * BLOCK, :]

    @pl.when(i < nct)
    def _():
        run(kc_ref[...], vc_ref[...], [None] * (kc_ref.shape[1] // LANE))

    @pl.when(i >= nct)
    def _():
        li = i - nct
        r = lax.broadcasted_iota(jnp.int32, (BLOCK, BLOCK), 0)
        c = lax.broadcasted_iota(jnp.int32, (BLOCK, BLOCK), 1)
        bias_p = jnp.where((c >= r) & (li >= 1), 0.0, NEG_INF)
        bias_n = jnp.where((c <= r) & ((li + 2) * BLOCK <= n_lat), 0.0, NEG_INF)
        kcat = jnp.concatenate([kc_ref[...], kp_ref[...], kq_ref[...], kn_ref[...]], axis=1)
        vcat = jnp.concatenate([vc_ref[...], vp_ref[...], vq_ref[...], vn_ref[...]], axis=0)
        run(kcat, vcat, [None] * (kc_ref.shape[1] // LANE) + [bias_p, None, bias_n])


def _attention(q, kt, v, sink_b, n_ctx, lat_only):
    B, S, _ = q.shape
    nct = n_ctx // BLOCK
    ntiles = S // BLOCK
    tile0 = nct if lat_only else 0
    rows_out = S - tile0 * BLOCK
    nunit = 2 * len(ATTN_UNIT_PAIRS)
    unit_rows = len(ATTN_UNIT_PAIRS[0]) * BLOCK
    cur = lambda i: i + tile0
    prev = lambda i: jnp.maximum(i + tile0 - 1, nct)
    nxt = lambda i: jnp.minimum(i + tile0 + 1, ntiles - 1)
    vblk = lambda f: pl.BlockSpec((None, BLOCK, KV_WIDTH), lambda b, i: (b, f(i), 0))
    kblk = lambda f: pl.BlockSpec((None, KV_WIDTH, BLOCK), lambda b, i: (b, 0, f(i)))
    return pl.pallas_call(
        functools.partial(_attn_kernel, nct=nct, n_lat=S - n_ctx, tile0=tile0),
        out_shape=jax.ShapeDtypeStruct((B, rows_out, ATTN_WIDTH), BF16),
        grid=(B, ntiles - tile0),
        in_specs=[
            pl.BlockSpec((None, BLOCK, ATTN_WIDTH), lambda b, i: (b, cur(i), 0)),
            pl.BlockSpec((None, KV_WIDTH, n_ctx), lambda b, i: (b, 0, 0)),
            pl.BlockSpec((None, n_ctx, KV_WIDTH), lambda b, i: (b, 0, 0)),
            kblk(prev), kblk(cur), kblk(nxt),
            vblk(prev), vblk(cur), vblk(nxt),
            pl.BlockSpec((N_Q_HEADS, LANE), lambda b, i: (0, 0)),
        ],
        out_specs=pl.BlockSpec((None, BLOCK, ATTN_WIDTH), lambda b, i: (b, i, 0)),
        scratch_shapes=[pltpu.VMEM((nunit, unit_rows, n_ctx + 3 * BLOCK), F32),
                        pltpu.VMEM((nunit, unit_rows, n_ctx + 3 * BLOCK), BF16)],
        compiler_params=_cparams(("parallel", "parallel")),
        name="attention",
    )(q, kt, v, kt, kt, kt, v, v, v, sink_b)


def _out_proj_kernel(a_ref, b_ref, c_ref, x_ref, mod_ref, wa_ref, wb_ref, wc_ref, gpost_ref, gpre_ref,
                     x1_ref, h2_ref, tmp_ref):
    y = (jnp.dot(a_ref[...], wa_ref[...], preferred_element_type=F32)
         + jnp.dot(b_ref[...], wb_ref[...], preferred_element_type=F32)
         + jnp.dot(c_ref[...], wc_ref[...], preferred_element_type=F32))
    x1 = x_ref[...] + mod_ref[2:3, :] * _rms(y, gpost_ref[...])
    x1_ref[...] = x1
    h2 = _rms(x1, gpre_ref[...]) * (1.0 + mod_ref[4:5, :]) + mod_ref[3:4, :]
    nl = tmp_ref.shape[0]
    T = h2.shape[0]
    G = T // SUBLANES
    pitch = G + SUBLANES
    for c in range(nl):
        for s in range(SUBLANES):
            tmp_ref[c, s * pitch:s * pitch + G, :] = h2[s * G:(s + 1) * G, c * LANE:(c + 1) * LANE]
    pack = 2
    for v in range(0, G, pack):
        blk = jnp.concatenate(
            [jnp.concatenate([tmp_ref[c, pl.ds(v + d, SUBLANES, stride=pitch), :] for c in range(nl)], axis=1)
             for d in range(pack)], axis=0)
        h2_ref[v * SUBLANES:(v + pack) * SUBLANES, :] = blk.astype(BF16)


def _out_proj(a, bs, att, xc, mod, wa, wb, wc, gpost, gpre, n_ctx, lat_only):
    B, S, _ = xc.shape
    T = TOK_TILE
    nct = n_ctx // T
    tile0 = nct if lat_only else 0
    rows_out = S - tile0 * T
    full = lambda width: pl.BlockSpec((None, T, width), lambda b, i: (b, i + tile0, 0))
    outs = lambda width: pl.BlockSpec((None, T, width), lambda b, i: (b, i, 0))
    const = lambda shape: pl.BlockSpec(shape, lambda b, i: (0,) * len(shape))
    return pl.pallas_call(
        _out_proj_kernel,
        out_shape=(jax.ShapeDtypeStruct((B, rows_out, D_MODEL), F32),
                   jax.ShapeDtypeStruct((B, rows_out, D_MODEL), BF16)),
        grid=(B, S // T - tile0),
        in_specs=[
            full(POOL_WIDTH),
            pl.BlockSpec((T, SSM_WIDTH), lambda b, i: (i + tile0, b)),
            outs(ATTN_WIDTH),
            full(D_MODEL),
            pl.BlockSpec((None, None, 6, D_MODEL), lambda b, i: (b, jnp.where(i + tile0 >= nct, 1, 0), 0, 0)),
            const((POOL_WIDTH, D_MODEL)), const((SSM_WIDTH, D_MODEL)), const((ATTN_WIDTH, D_MODEL)),
            const((1, D_MODEL)), const((1, D_MODEL)),
        ],
        out_specs=(outs(D_MODEL), outs(D_MODEL)),
        scratch_shapes=[pltpu.VMEM((D_MODEL // LANE, T + SUBLANES * SUBLANES, LANE), F32)],
        compiler_params=_cparams(("parallel", "parallel")),
        name="out_proj",
    )(a, bs, att, xc, mod, wa, wb, wc, gpost, gpre)


def _ffn_kernel(h_ref, hp_ref, hn_ref, x_ref, mod_ref, wu_ref, cw_ref, cb_ref, wd_ref, g_ref,
                o_ref, act_ref, y_ref, *, seq_starts, seq_ends):
    i = pl.program_id(1)
    T = h_ref.shape[0]
    G = T // SUBLANES
    first = functools.reduce(jnp.logical_or, [i == s for s in seq_starts])
    last = functools.reduce(jnp.logical_or, [i == e for e in seq_ends])
    half = HALO_ROWS // 2
    lhs = jnp.concatenate([h_ref[...], hp_ref[half:, :], hn_ref[:half, :]], axis=0)
    sub = lax.broadcasted_iota(jnp.int32, (SUBLANES, 2 * FF_CHUNK), 0)

    for j in range(wu_ref.shape[0]):
        ue = jnp.dot(lhs, wu_ref[j], preferred_element_type=F32)
        u = ue[:T, :]
        edge = ue[T:, :]
        prev_row = jnp.where(first, 0.0, edge[half - 1:half, :])
        next_row = jnp.where(last, 0.0, edge[half:half + 1, :])
        head = jnp.where(sub == 0, prev_row, pltpu.roll(u[T - SUBLANES:, :], 1, axis=0))
        tail = jnp.where(sub == SUBLANES - 1, next_row, pltpu.roll(u[:SUBLANES, :], SUBLANES - 1, axis=0))
        up = jnp.concatenate([head, u[:T - SUBLANES, :]], axis=0)
        un = jnp.concatenate([u[SUBLANES:, :], tail], axis=0)
        cw = cw_ref[j]
        c = cw[0:1, :] * up + cw[1:2, :] * u + cw[2:3, :] * un + cb_ref[j]
        val = c[:, :FF_CHUNK]
        gate = c[:, FF_CHUNK:]
        act = gate * _sigmoid(gate) * val
        act_ref[:, j * FF_CHUNK:(j + 1) * FF_CHUNK] = act.astype(BF16)
    y = jnp.dot(act_ref[...], wd_ref[...], preferred_element_type=F32)
    r = _rms(y, g_ref[...])
    nl = y_ref.shape[0]
    for c in range(nl):
        y_ref[c] = r[:, c * LANE:(c + 1) * LANE]
    gt = mod_ref[5:6, :]
    for k in range(G):
        src = pl.ds((SUBLANES * k % G) * SUBLANES + SUBLANES * k // G, SUBLANES, stride=SUBLANES)
        rows = slice(k * SUBLANES, (k + 1) * SUBLANES)
        yk = jnp.concatenate([y_ref[c, src, :] for c in range(nl)], axis=1)
        o_ref[rows, :] = x_ref[rows, :] + gt * yk


def _ffn(h2, x1, mod, wu, cw, cb, wd, g, n_ctx, lat_only):
    B, rows, _ = x1.shape
    T = TOK_TILE
    ntiles = rows // T
    nct = 0 if lat_only else n_ctx // T
    seq_starts = (0,) if lat_only else (0, nct)
    seq_ends = (ntiles - 1,) if lat_only else (nct - 1, ntiles - 1)
    hb = T // HALO_ROWS
    nhb = rows // HALO_ROWS
    nch = wu.shape[0]
    tok = lambda width: pl.BlockSpec((None, T, width), lambda b, i: (b, i, 0))
    const = lambda shape: pl.BlockSpec(shape, lambda b, i: (0,) * len(shape))
    return pl.pallas_call(
        functools.partial(_ffn_kernel, seq_starts=seq_starts, seq_ends=seq_ends),
        out_shape=jax.ShapeDtypeStruct((B, rows, D_MODEL), F32),
        grid=(B, ntiles),
        in_specs=[
            tok(D_MODEL),
            pl.BlockSpec((None, HALO_ROWS, D_MODEL), lambda b, i: (b, jnp.maximum(i * hb - 1, 0), 0)),
            pl.BlockSpec((None, HALO_ROWS, D_MODEL), lambda b, i: (b, jnp.minimum((i + 1) * hb, nhb - 1), 0)),
            tok(D_MODEL),
            pl.BlockSpec((None, None, 6, D_MODEL),
                         lambda b, i: (b, jnp.where(i >= nct, 1, 0), 0, 0)),
            const((nch, D_MODEL, 2 * FF_CHUNK)),
            const((nch, 3, 2 * FF_CHUNK)),
            const((nch, 1, 2 * FF_CHUNK)),
            const((D_FF, D_MODEL)),
            const((1, D_MODEL)),
        ],
        out_specs=tok(D_MODEL),
        scratch_shapes=[pltpu.VMEM((T, D_FF), BF16), pltpu.VMEM((D_MODEL // LANE, T, LANE), F32)],
        compiler_params=_cparams(("parallel", "parallel")),
        name="conv_ffn",
    )(h2, h2, h2, x1, mod, wu, cw, cb, wd, g)


def _zoh_blocks(lam_re, lam_im, log_dt, b_re, b_im, c_re, c_im):
    lr = jnp.minimum(lam_re, -1e-4)
    li = lam_im
    dt = jnp.exp(log_dt)[..., None]
    mag = jnp.exp(lr * dt)
    lbr = mag * jnp.cos(li * dt)
    lbi = mag * jnp.sin(li * dt)
    den = lr * lr + li * li
    fr = ((lbr - 1.0) * lr + lbi * li) / den
    fi = (lbi * lr - (lbr - 1.0) * li) / den
    bbr = fr[..., None] * b_re - fi[..., None] * b_im
    bbi = fr[..., None] * b_im + fi[..., None] * b_re
    eye = jnp.eye(SSM_GROUPS, dtype=F32)
    blk_b = lambda t: jnp.einsum('dgph,gk->dghkp', t, eye).reshape(2, SSM_WIDTH, SSM_LANES)
    blk_c = lambda t: jnp.einsum('dghp,gk->dgpkh', t, eye).reshape(2, SSM_LANES, SSM_WIDTH)
    bcat = jnp.concatenate([blk_b(bbr), blk_b(bbi)], axis=-1).astype(BF16)
    ccat = jnp.concatenate([blk_c(c_re), -blk_c(c_im)], axis=1).astype(BF16)
    lam = jnp.concatenate([lbr.reshape(2, 1, SSM_LANES), lbi.reshape(2, 1, SSM_LANES)], axis=-1)
    return bcat, lam, ccat


def _head_perm():
    cols = []
    for p in range(Q_PER_KV):
        for h in (p, Q_PER_KV + p):
            cols.extend(range(h * HEAD_DIM, (h + 1) * HEAD_DIM))
    return jnp.array(cols, dtype=jnp.int32)


def _rope_tables(n_ctx, n_lat):
    t = jnp.arange(n_lat)
    lane = jnp.arange(LANE)
    hl = lane % HEAD_DIM
    quarter = HEAD_DIM // 4
    inv_freq = ROPE_BASE ** (-jnp.arange(quarter, dtype=F32) / quarter)
    pos = jnp.where(hl[None, :] < HEAD_DIM // 2, (t // GRID_W)[:, None], (t % GRID_W)[:, None])
    ang = pos.astype(F32) * inv_freq[hl % quarter][None, :]
    first_half = (hl % (2 * quarter)) < quarter
    cos_l = jnp.cos(ang)
    sin_l = jnp.where(first_half[None, :], -jnp.sin(ang), jnp.sin(ang))
    cos_t = jnp.concatenate([jnp.ones((n_ctx, LANE), F32), cos_l], axis=0)
    sin_t = jnp.concatenate([jnp.zeros((n_ctx, LANE), F32), sin_l], axis=0)
    return cos_t, sin_t


def kernel(x, c, ctx, c_ctx, w_ada, b_ada, g_pre_mix, g_post_mix, g_pre_ffn, g_post_ffn, w_in, pool_w, pool_scale, lam_re, lam_im, log_dt, b_re, b_im, c_re, c_im, s5_d, w_glu, b_glu, sink, w_out, w_up, conv_w, conv_b, w_down):
    B, L, _ = x.shape
    N = ctx.shape[1]
    S = N + L
    assert N % TOK_TILE == 0 and L % TOK_TILE == 0 and L % GRID_W == 0

    rows = -(-(B + 1) // 8) * 8
    cond = jnp.concatenate([c, c_ctx[None, :], jnp.zeros((rows - B - 1, D_MODEL), F32)], axis=0)
    mod_all = _ada(cond, w_ada, b_ada)
    mod_lat = mod_all[:, :B].reshape(DEPTH, B, 1, 6, D_MODEL)
    mod_ctx = jnp.broadcast_to(mod_all[:, B].reshape(DEPTH, 1, 1, 6, D_MODEL), (DEPTH, B, 1, 6, D_MODEL))
    mods = jnp.concatenate([mod_ctx, mod_lat], axis=2)

    cos_t, sin_t = _rope_tables(N, L)
    perm = _head_perm()
    q0 = POOL_WIDTH + SSM_WIDTH
    in_cols = jnp.concatenate([jnp.arange(q0), q0 + perm, jnp.arange(q0 + ATTN_WIDTH, D_IN)])
    out_rows = jnp.concatenate([jnp.arange(q0), q0 + perm])
    eye_w = jnp.eye(len(POOL_WINDOWS), dtype=F32)
    nch = D_FF // FF_CHUNK

    def ffn_chunks(t):
        r = t.shape[0]
        return t.reshape(r, 2, nch, FF_CHUNK).transpose(2, 0, 1, 3).reshape(nch, r, 2 * FF_CHUNK)

    xc = jnp.concatenate([ctx, x], axis=1)
    for l in range(DEPTH):
        last = l == DEPTH - 1
        w_in_l = w_in[l][:, in_cols].astype(BF16)
        w_out_l = w_out[l][out_rows].astype(BF16)
        pool_blk = jnp.einsum('gcd,gk->gckd', pool_w[l], eye_w).reshape(POOL_WIDTH, POOL_WIDTH).astype(BF16)
        bcat, lam, ccat = _zoh_blocks(lam_re[l], lam_im[l], log_dt[l], b_re[l], b_im[l], c_re[l], c_im[l])
        sink_b = jnp.broadcast_to(sink[l][:, None] * LOG2_E, (N_Q_HEADS, LANE))
        wu = ffn_chunks(w_up[l]).astype(BF16)
        cw = ffn_chunks(conv_w[l])
        cb = ffn_chunks(conv_b[l][None, :])
        wd = w_down[l].astype(BF16)

        pool_in, us, q, kt, v = _in_proj(xc, mods[l], g_pre_mix[l][None, :], w_in_l, cos_t, sin_t, N // TOK_TILE)
        a = _pool(pool_in, pool_blk, pool_scale[l][None, :], N)
        bs = _s5(us.reshape(S * B, SSM_WIDTH), bcat, lam, ccat, s5_d[l][None, :],
                 w_glu[l].astype(BF16), b_glu[l][None, :], B, N)
        att = _attention(q, kt, v, sink_b, N, last)
        x1, h2 = _out_proj(a, bs.reshape(S, B * SSM_WIDTH), att, xc, mods[l],
                           w_out_l[:POOL_WIDTH], w_out_l[POOL_WIDTH:q0], w_out_l[q0:],
                           g_post_mix[l][None, :], g_pre_ffn[l][None, :], N, last)
        xc = _ffn(h2, x1, mods[l], wu, cw, cb, wd, g_post_ffn[l][None, :], N, last)
    return xc
```

```python
import functools
import math

import jax
import jax.numpy as jnp
from jax import lax
from jax.experimental import pallas as pl
from jax.experimental.pallas import tpu as pltpu

F32 = jnp.float32
BF16 = jnp.bfloat16

D_MODEL = 1024
DEPTH = 4
GRID_W = 64
EPS = 1e-6
POOL_WINDOWS = (2, 4, 8, 16)
POOL_WIDTH = 256
POOL_GROUP = 64
SSM_WIDTH = 256
SSM_CH = 16
SSM_GROUPS = 16
SSM_STATE = 64
SSM_LANES = SSM_GROUPS * SSM_STATE
HEAD_DIM = 64
ATTN_WIDTH = 512
N_Q_HEADS = 8
N_KV_HEADS = 2
Q_PER_KV = 4
KV_WIDTH = 128
WINDOW = 128
BLOCK = 128
ROPE_BASE = 10000.0
D_FF = 2816
D_IN = 1280
NEG_INF = -1e30
LOG2_E = 1.4426950408889634

LANE = 128
POOL_PAD = 16
POOL_HALO = 8
POOL_CHUNK = 128
TOK_TILE = 256
S5_STEPS = 16
FF_CHUNK = 256
HALO_ROWS = 16
SUBLANES = 8
IN_SPLIT = 2
ATTN_UNIT_PAIRS = ((0, 1, 2, 3),)
VMEM_LIMIT = 56 * 1024 * 1024


def _cparams(sem):
    return pltpu.CompilerParams(dimension_semantics=sem, vmem_limit_bytes=VMEM_LIMIT)


def _rms(x, g):
    return x * lax.rsqrt(jnp.mean(x * x, axis=-1, keepdims=True) + EPS) * g


def _sigmoid(x):
    return 1.0 / (1.0 + jnp.exp(-x))


def _gelu_tanh(x):
    c = math.sqrt(2.0 / math.pi)
    return 0.5 * x * (1.0 + jnp.tanh(c * (x + 0.044715 * (x * x * x))))


def _ada_kernel(c_ref, w_ref, b_ref, o_ref):
    c = c_ref[...]
    s = c * _sigmoid(c)
    o_ref[...] = jnp.dot(s.astype(BF16), w_ref[...].astype(BF16), preferred_element_type=F32) + b_ref[...]


def _ada(cond, w_ada, b_ada):
    rows = cond.shape[0]
    nblk = w_ada.shape[-1] // D_MODEL
    return pl.pallas_call(
        _ada_kernel,
        out_shape=jax.ShapeDtypeStruct((DEPTH, rows, nblk * D_MODEL), F32),
        grid=(DEPTH, nblk),
        in_specs=[
            pl.BlockSpec((rows, D_MODEL), lambda l, j: (0, 0)),
            pl.BlockSpec((None, D_MODEL, D_MODEL), lambda l, j: (l, 0, j)),
            pl.BlockSpec((None, 1, D_MODEL), lambda l, j: (l, 0, j)),
        ],
        out_specs=pl.BlockSpec((None, rows, D_MODEL), lambda l, j: (l, 0, j)),
        compiler_params=_cparams(("arbitrary", "arbitrary")),
        name="ada_mod",
    )(cond, w_ada, b_ada.reshape(DEPTH, 1, -1))


def _in_proj_kernel(x_ref, mod_ref, g_ref, w_ref, cos_ref, sin_ref,
                    pool_ref, us_ref, q_ref, kt_ref, v_ref):
    T = x_ref.shape[0]
    lane = lax.broadcasted_iota(jnp.int32, (T // IN_SPLIT, LANE), 1)
    first_half = (lane % 32) < 16
    q0 = POOL_WIDTH + SSM_WIDTH
    k0 = q0 + ATTN_WIDTH
    scale = HEAD_DIM ** -0.5 * LOG2_E
    w = w_ref[...]
    for part in range(IN_SPLIT):
        rows = slice(part * (T // IN_SPLIT), (part + 1) * (T // IN_SPLIT))
        h = _rms(x_ref[rows, :], g_ref[...]) * (1.0 + mod_ref[1:2, :]) + mod_ref[0:1, :]
        z = jnp.dot(h.astype(BF16), w, preferred_element_type=F32)
        pool_ref[rows, :] = z[:, 0:POOL_WIDTH]
        us_ref[rows, :] = z[:, POOL_WIDTH:POOL_WIDTH + SSM_WIDTH]
        cos = cos_ref[rows, :]
        sin = sin_ref[rows, :]

        def rope(t):
            partner = jnp.where(first_half, pltpu.roll(t, LANE - 16, axis=1), pltpu.roll(t, 16, axis=1))
            return t * cos + partner * sin

        for j in range(ATTN_WIDTH // LANE):
            t = z[:, q0 + j * LANE:q0 + (j + 1) * LANE]
            q_ref[rows, j * LANE:(j + 1) * LANE] = (rope(t) * scale).astype(BF16)
        kt_ref[:, rows] = rope(z[:, k0:k0 + KV_WIDTH]).T.astype(BF16)
        v_ref[rows, :] = z[:, k0 + KV_WIDTH:k0 + 2 * KV_WIDTH].astype(BF16)


def _in_proj(xc, mod, g, w_in, cos_t, sin_t, nct):
    B, S, _ = xc.shape
    T = TOK_TILE
    grid = (B, S // T)
    tok = lambda width: pl.BlockSpec((None, T, width), lambda b, i: (b, i, 0))
    return pl.pallas_call(
        _in_proj_kernel,
        out_shape=(
            jax.ShapeDtypeStruct((B, S, POOL_WIDTH), F32),
            jax.ShapeDtypeStruct((S, B * SSM_WIDTH), F32),
            jax.ShapeDtypeStruct((B, S, ATTN_WIDTH), BF16),
            jax.ShapeDtypeStruct((B, KV_WIDTH, S), BF16),
            jax.ShapeDtypeStruct((B, S, KV_WIDTH), BF16),
        ),
        grid=grid,
        in_specs=[
            tok(D_MODEL),
            pl.BlockSpec((None, None, 6, D_MODEL), lambda b, i: (b, jnp.where(i >= nct, 1, 0), 0, 0)),
            pl.BlockSpec((1, D_MODEL), lambda b, i: (0, 0)),
            pl.BlockSpec((D_MODEL, D_IN), lambda b, i: (0, 0)),
            pl.BlockSpec((T, LANE), lambda b, i: (i, 0)),
            pl.BlockSpec((T, LANE), lambda b, i: (i, 0)),
        ],
        out_specs=(
            tok(POOL_WIDTH),
            pl.BlockSpec((T, SSM_WIDTH), lambda b, i: (i, b)),
            tok(ATTN_WIDTH),
            pl.BlockSpec((None, KV_WIDTH, T), lambda b, i: (b, 0, i)),
            tok(KV_WIDTH),
        ),
        compiler_params=_cparams(("parallel", "parallel")),
        name="in_proj",
    )(xc, mod, g, w_in, cos_t, sin_t)


def _pool_kernel(u_ref, w_ref, sc_ref, o_ref, pad_ref, *, n_ctx, n_lat):
    C = POOL_CHUNK
    zeros = jnp.zeros((POOL_PAD, POOL_WIDTH), F32)
    ctx0 = POOL_PAD
    lat0 = 2 * POOL_PAD + n_ctx
    pad_ref[0:POOL_PAD, :] = zeros
    pad_ref[ctx0 + n_ctx:lat0, :] = zeros
    pad_ref[lat0 + n_lat:lat0 + n_lat + POOL_PAD, :] = zeros
    pad_ref[ctx0:ctx0 + n_ctx, :] = u_ref[0:n_ctx, :]
    pad_ref[lat0:lat0 + n_lat, :] = u_ref[n_ctx:n_ctx + n_lat, :]

    nw = len(POOL_WINDOWS)
    t_idx = lax.broadcasted_iota(jnp.int32, (C, C + 2 * POOL_HALO), 0)
    s_idx = lax.broadcasted_iota(jnp.int32, (C, C + 2 * POOL_HALO), 1) - POOL_HALO
    band = jnp.concatenate(
        [jnp.where((s_idx >= t_idx - w // 2) & (s_idx < t_idx - w // 2 + w), 1.0, 0.0).astype(BF16)
         for w in POOL_WINDOWS], axis=0)
    grp = lax.broadcasted_iota(jnp.int32, (C, POOL_WIDTH), 1) // POOL_GROUP
    row = lax.broadcasted_iota(jnp.int32, (C, POOL_WIDTH), 0)
    w_lane = functools.reduce(lambda acc, gw: jnp.where(grp == gw[0], gw[1], acc),
                              list(enumerate(POOL_WINDOWS))[1:], jnp.full((C, POOL_WIDTH), POOL_WINDOWS[0]))
    inv_w = 1.0 / w_lane.astype(F32)
    wmat = w_ref[...]
    scale = sc_ref[...]

    def window_sums(c, pad_base):
        base = pad_base + c * C
        win = pad_ref[base - POOL_HALO:base + C + POOL_HALO, :]
        hi = win.astype(BF16)
        lo = (win - hi.astype(F32)).astype(BF16)
        return (jnp.dot(band, hi, preferred_element_type=F32)
                + jnp.dot(band, lo, preferred_element_type=F32))

    def finish(sums, c, pad_base, out_base, seq_len):
        base = pad_base + c * C
        pooled = sums[0:C, :]
        for gi in range(1, nw):
            pooled = jnp.where(grp == gi, sums[gi * C:(gi + 1) * C, :], pooled)
        if c * C >= POOL_HALO and (c + 1) * C + POOL_HALO <= seq_len:
            pooled = pooled * inv_w
        else:
            start = row + c * C - w_lane // 2
            cnt = jnp.minimum(start + w_lane, seq_len) - jnp.maximum(start, 0)
            pooled = pooled / cnt.astype(F32)
        pooled = pooled - pad_ref[base:base + C, :]
        mixed = jnp.dot(pooled.astype(BF16), wmat, preferred_element_type=F32) * scale
        o_ref[out_base + c * C:out_base + (c + 1) * C, :] = mixed.astype(BF16)

    jobs = ([(c, ctx0, 0, n_ctx) for c in range(n_ctx // C)]
            + [(c, lat0, n_ctx, n_lat) for c in range(n_lat // C)])
    sums = window_sums(jobs[0][0], jobs[0][1])
    for k, job in enumerate(jobs):
        nxt = window_sums(jobs[k + 1][0], jobs[k + 1][1]) if k + 1 < len(jobs) else None
        finish(sums, *job)
        sums = nxt


def _pool(pool_in, w_blk, scale, n_ctx):
    B, S, _ = pool_in.shape
    n_lat = S - n_ctx
    return pl.pallas_call(
        functools.partial(_pool_kernel, n_ctx=n_ctx, n_lat=n_lat),
        out_shape=jax.ShapeDtypeStruct((B, S, POOL_WIDTH), BF16),
        grid=(B,),
        in_specs=[
            pl.BlockSpec((None, S, POOL_WIDTH), lambda b: (b, 0, 0)),
            pl.BlockSpec((POOL_WIDTH, POOL_WIDTH), lambda b: (0, 0)),
            pl.BlockSpec((1, POOL_WIDTH), lambda b: (0, 0)),
        ],
        out_specs=pl.BlockSpec((None, S, POOL_WIDTH), lambda b: (b, 0, 0)),
        scratch_shapes=[pltpu.VMEM((S + 3 * POOL_PAD, POOL_WIDTH), F32)],
        compiler_params=_cparams(("parallel",)),
        name="pool_mixer",
    )(pool_in, w_blk, scale)


def _s5_scan(lam_ref, bu_ref, hb_ref, state, *, batch, reverse):
    for cb in range(SSM_LANES // LANE):
        re = slice(cb * LANE, (cb + 1) * LANE)
        im = slice(SSM_LANES + cb * LANE, SSM_LANES + (cb + 1) * LANE)
        lr = jnp.broadcast_to(lam_ref[0:1, re], (batch, LANE))
        li = jnp.broadcast_to(lam_ref[0:1, im], (batch, LANE))
        s_re = state[:, re]
        s_im = state[:, im]
        for step in range(S5_STEPS):
            t = S5_STEPS - 1 - step if reverse else step
            rows = slice(t * batch, (t + 1) * batch)
            n_re = lr * s_re - li * s_im + bu_ref[rows, re]
            n_im = lr * s_im + li * s_re + bu_ref[rows, im]
            hb_ref[rows, re] = n_re.astype(BF16)
            hb_ref[rows, im] = n_im.astype(BF16)
            s_re, s_im = n_re, n_im
        state[:, re] = s_re
        state[:, im] = s_im


def _s5_pair(u_ref, bcat_ref, lam_ref, ccat_ref, bu0, bu1, hb0, hb1, state, *, batch, reverse):
    R = S5_STEPS * batch

    @pl.when(pl.program_id(0) == 0)
    def _():
        state[...] = jnp.zeros(state.shape, F32)
        for ref in (bu0, bu1, hb0, hb1):
            ref[...] = jnp.zeros(ref.shape, ref.dtype)

    first, second = (slice(R, 2 * R), slice(0, R)) if reverse else (slice(0, R), slice(R, 2 * R))
    bcat = bcat_ref[...]
    ccat = ccat_ref[...]
    y_a = jnp.dot(hb0[...], ccat, preferred_element_type=F32)
    bu0[...] = jnp.dot(u_ref[first, :].astype(BF16), bcat, preferred_element_type=F32)
    _s5_scan(lam_ref, bu1, hb1, state, batch=batch, reverse=reverse)
    bu1[...] = jnp.dot(u_ref[second, :].astype(BF16), bcat, preferred_element_type=F32)
    y_b = jnp.dot(hb1[...], ccat, preferred_element_type=F32)
    _s5_scan(lam_ref, bu0, hb0, state, batch=batch, reverse=reverse)
    return jnp.concatenate([y_b, y_a] if reverse else [y_a, y_b], axis=0)


def _s5_fwd_kernel(u_ref, bcat_ref, lam_ref, ccat_ref, y_ref, bu0, bu1, hb0, hb1, state, *, batch):
    y_ref[...] = _s5_pair(u_ref, bcat_ref, lam_ref, ccat_ref, bu0, bu1, hb0, hb1, state,
                          batch=batch, reverse=False)


def _s5_bwd_kernel(u_ref, bcat_ref, lam_ref, ccat_ref, u2_ref, yf_ref, dsk_ref, wglu_ref, bglu_ref,
                   o_ref, bu0, bu1, hb0, hb1, state, *, batch):
    y = _s5_pair(u_ref, bcat_ref, lam_ref, ccat_ref, bu0, bu1, hb0, hb1, state, batch=batch, reverse=True)
    tot = yf_ref[...] + y + dsk_ref[...] * u2_ref[...]
    g = _gelu_tanh(tot)
    gate = jnp.dot(g.astype(BF16), wglu_ref[...], preferred_element_type=F32) + bglu_ref[...]
    o_ref[...] = (g * _sigmoid(gate)).astype(BF16)


def _s5(us_tm, bcat, lam, ccat, dskip, w_glu, b_glu, batch, n_ctx):
    rows_total = us_tm.shape[0]
    R = S5_STEPS * batch
    nch = rows_total // (2 * R)
    nctx = n_ctx // (2 * S5_STEPS)
    assert rows_total % (2 * R) == 0 and n_ctx % (2 * S5_STEPS) == 0
    const = lambda shape: pl.BlockSpec(shape, lambda j: (0,) * len(shape))

    def dir_spec(d, shape):
        return pl.BlockSpec((None,) + shape, lambda j: (d,) + (0,) * len(shape))

    def chunk_spec(order, lag):
        return pl.BlockSpec((2 * R, SSM_WIDTH), lambda j: (order(jnp.clip(j - lag, 0, nch - 1)), 0))

    fwd = lambda c: c
    rev = lambda c: jnp.where(c < nctx, nctx - 1 - c, nch + nctx - 1 - c)

    scratch = [pltpu.VMEM((R, 2 * SSM_LANES), F32), pltpu.VMEM((R, 2 * SSM_LANES), F32),
               pltpu.VMEM((R, 2 * SSM_LANES), BF16), pltpu.VMEM((R, 2 * SSM_LANES), BF16),
               pltpu.VMEM((batch, 2 * SSM_LANES), F32)]
    yf = pl.pallas_call(
        functools.partial(_s5_fwd_kernel, batch=batch),
        out_shape=jax.ShapeDtypeStruct((rows_total, SSM_WIDTH), F32),
        grid=(nch + 1,),
        in_specs=[
            chunk_spec(fwd, 0),
            dir_spec(0, (SSM_WIDTH, 2 * SSM_LANES)),
            dir_spec(0, (1, 2 * SSM_LANES)),
            dir_spec(0, (2 * SSM_LANES, SSM_WIDTH)),
        ],
        out_specs=chunk_spec(fwd, 1),
        scratch_shapes=scratch,
        compiler_params=_cparams(("arbitrary",)),
        name="s5_forward",
    )(us_tm, bcat, lam, ccat)

    return pl.pallas_call(
        functools.partial(_s5_bwd_kernel, batch=batch),
        out_shape=jax.ShapeDtypeStruct((rows_total, SSM_WIDTH), BF16),
        grid=(nch + 1,),
        in_specs=[
            chunk_spec(rev, 0),
            dir_spec(1, (SSM_WIDTH, 2 * SSM_LANES)),
            dir_spec(1, (1, 2 * SSM_LANES)),
            dir_spec(1, (2 * SSM_LANES, SSM_WIDTH)),
            chunk_spec(rev, 1),
            chunk_spec(rev, 1),
            const((1, SSM_WIDTH)),
            const((SSM_WIDTH, SSM_WIDTH)),
            const((1, SSM_WIDTH)),
        ],
        out_specs=chunk_spec(rev, 1),
        scratch_shapes=scratch,
        compiler_params=_cparams(("arbitrary",)),
        name="s5_backward",
    )(us_tm, bcat, lam, ccat, us_tm, yf, dskip, w_glu, b_glu)


def _attn_kernel(q_ref, kc_ref, vc_ref, kp_ref, kq_ref, kn_ref, vp_ref, vq_ref, vn_ref, sink_ref,
                 o_ref, s_ref, e_ref, *, nct, n_lat, tile0):
    i = pl.program_id(1) + tile0
    low = lax.broadcasted_iota(jnp.int32, (BLOCK, LANE), 1) < HEAD_DIM

    def run(kcat, vcat, biases):
        low_v = lax.broadcasted_iota(jnp.int32, vcat.shape, 1) < HEAD_DIM
        one = jnp.ones_like(vcat)
        vones = (jnp.where(low_v, vcat, one), jnp.where(low_v, one, vcat))
        npair = ATTN_WIDTH // LANE
        nk = kcat.shape[1]
        qs = [q_ref[:, p * LANE:(p + 1) * LANE] for p in range(npair)]
        zero = jnp.zeros_like(qs[0])
        units = [(half, pairs) for pairs in ATTN_UNIT_PAIRS for half in (0, 1)]
        per_unit = len(ATTN_UNIT_PAIRS[0])
        stacked = [None if b is None else jnp.concatenate([b] * per_unit, axis=0) for b in biases]

        def scores(u):
            half, pairs = units[u]
            q = jnp.concatenate([jnp.where(low, qs[p], zero) if half == 0 else jnp.where(low, zero, qs[p])
                                 for p in pairs], axis=0)
            s_ref[u, :, :nk] = jnp.dot(q, kcat, preferred_element_type=F32)

        def softmax(u):
            half, pairs = units[u]
            sink = jnp.concatenate(
                [jnp.broadcast_to(sink_ref[half * Q_PER_KV + p:half * Q_PER_KV + p + 1, 0:1], (BLOCK, 1))
                 for p in pairs], axis=0)
            tiles = []
            for t, bias in enumerate(stacked):
                st = s_ref[u, :, t * LANE:(t + 1) * LANE]
                tiles.append(st if bias is None else st + bias)
            mx = functools.reduce(jnp.maximum, tiles)
            m = jnp.maximum(jnp.max(mx, axis=-1, keepdims=True), sink)
            for t, st in enumerate(tiles):
                e_ref[u, :, t * LANE:(t + 1) * LANE] = jnp.exp2(st - m).astype(BF16)
            return jnp.exp2(sink - m)

        def values(u):
            return jnp.dot(e_ref[u, :, :nk], vones[units[u][0]], preferred_element_type=F32)

        nu = len(units)
        sink_terms, accs = [None] * nu, [None] * nu
        scores(0)
        scores(1)
        for u in range(nu):
            sink_terms[u] = softmax(u)
            if u + 2 < nu:
                scores(u + 2)
            accs[u] = values(u)
        low2 = jnp.concatenate([low] * per_unit, axis=0)
        for g in range(nu // 2):
            a_lo, a_hi = accs[2 * g], accs[2 * g + 1]
            num = jnp.where(low2, a_lo, a_hi)
            den = (jnp.where(low2, pltpu.roll(a_lo, HEAD_DIM, axis=1), pltpu.roll(a_hi, HEAD_DIM, axis=1))
                   + jnp.where(low2, sink_terms[2 * g], sink_terms[2 * g + 1]))
            out = (num / den).astype(BF16)
            for r, p in enumerate(units[2 * g][1]):
                o_ref[:, p * LANE:(p + 1) * LANE] = out[r * BLOCK:(r + 1) * BLOCK, :]

    @pl.when(i < nct)
    def _():
        run(kc_ref[...], vc_ref[...], [None] * (kc_ref.shape[1] // LANE))

    @pl.when(i >= nct)
    def _():
        li = i - nct
        r = lax.broadcasted_iota(jnp.int32, (BLOCK, BLOCK), 0)
        c = lax.broadcasted_iota(jnp.int32, (BLOCK, BLOCK), 1)
        bias_p = jnp.where((c >= r) & (li >= 1), 0.0, NEG_INF)
        bias_n = jnp.where((c <= r) & ((li + 2) * BLOCK <= n_lat), 0.0, NEG_INF)
        kcat = jnp.concatenate([kc_ref[...], kp_ref[...], kq_ref[...], kn_ref[...]], axis=1)
        vcat = jnp.concatenate([vc_ref[...], vp_ref[...], vq_ref[...], vn_ref[...]], axis=0)
        run(kcat, vcat, [None] * (kc_ref.shape[1] // LANE) + [bias_p, None, bias_n])


def _attention(q, kt, v, sink_b, n_ctx, lat_only):
    B, S, _ = q.shape
    nct = n_ctx // BLOCK
    ntiles = S // BLOCK
    tile0 = nct if lat_only else 0
    rows_out = S - tile0 * BLOCK
    nunit = 2 * len(ATTN_UNIT_PAIRS)
    unit_rows = len(ATTN_UNIT_PAIRS[0]) * BLOCK
    cur = lambda i: i + tile0
    prev = lambda i: jnp.maximum(i + tile0 - 1, nct)
    nxt = lambda i: jnp.minimum(i + tile0 + 1, ntiles - 1)
    vblk = lambda f: pl.BlockSpec((None, BLOCK, KV_WIDTH), lambda b, i: (b, f(i), 0))
    kblk = lambda f: pl.BlockSpec((None, KV_WIDTH, BLOCK), lambda b, i: (b, 0, f(i)))
    return pl.pallas_call(
        functools.partial(_attn_kernel, nct=nct, n_lat=S - n_ctx, tile0=tile0),
        out_shape=jax.ShapeDtypeStruct((B, rows_out, ATTN_WIDTH), BF16),
        grid=(B, ntiles - tile0),
        in_specs=[
            pl.BlockSpec((None, BLOCK, ATTN_WIDTH), lambda b, i: (b, cur(i), 0)),
            pl.BlockSpec((None, KV_WIDTH, n_ctx), lambda b, i: (b, 0, 0)),
            pl.BlockSpec((None, n_ctx, KV_WIDTH), lambda b, i: (b, 0, 0)),
            kblk(prev), kblk(cur), kblk(nxt),
            vblk(prev), vblk(cur), vblk(nxt),
            pl.BlockSpec((N_Q_HEADS, LANE), lambda b, i: (0, 0)),
        ],
        out_specs=pl.BlockSpec((None, BLOCK, ATTN_WIDTH), lambda b, i: (b, i, 0)),
        scratch_shapes=[pltpu.VMEM((nunit, unit_rows, n_ctx + 3 * BLOCK), F32),
                        pltpu.VMEM((nunit, unit_rows, n_ctx + 3 * BLOCK), BF16)],
        compiler_params=_cparams(("parallel", "parallel")),
        name="attention",
    )(q, kt, v, kt, kt, kt, v, v, v, sink_b)


def _interleave_rows(h2, tmp_ref, dst_ref):
    nl = tmp_ref.shape[0]
    T = h2.shape[0]
    G = T // SUBLANES
    pitch = G + SUBLANES
    for c in range(nl):
        for s in range(SUBLANES):
            tmp_ref[c, s * pitch:s * pitch + G, :] = h2[s * G:(s + 1) * G, c * LANE:(c + 1) * LANE]
    pack = 2
    for v in range(0, G, pack):
        blk = jnp.concatenate(
            [jnp.concatenate([tmp_ref[c, pl.ds(v + d, SUBLANES, stride=pitch), :] for c in range(nl)], axis=1)
             for d in range(pack)], axis=0)
        dst_ref[v * SUBLANES:(v + pack) * SUBLANES, :] = blk.astype(BF16)


def _tail_kernel(a_ref, ap_ref, an_ref, b_ref, bp_ref, bn_ref, c_ref, cp_ref, cn_ref,
                 x_ref, xp_ref, xn_ref, mod_ref, wa_ref, wb_ref, wc_ref, gpost_ref, gpre_ref,
                 wu_ref, cw_ref, cb_ref, wd_ref, g_ref,
                 o_ref, act_ref, y_ref, tmp_ref, lhs_ref, x1_ref, *, seq_starts, seq_ends):
    i = pl.program_id(1)
    T = x_ref.shape[0]
    G = T // SUBLANES
    first = functools.reduce(jnp.logical_or, [i == s for s in seq_starts])
    last = functools.reduce(jnp.logical_or, [i == e for e in seq_ends])
    half = HALO_ROWS // 2

    def ext(ref, prev_ref, next_ref):
        return jnp.concatenate([ref[...], prev_ref[half:, :], next_ref[:half, :]], axis=0)

    y = (jnp.dot(ext(a_ref, ap_ref, an_ref), wa_ref[...], preferred_element_type=F32)
         + jnp.dot(ext(b_ref, bp_ref, bn_ref), wb_ref[...], preferred_element_type=F32)
         + jnp.dot(ext(c_ref, cp_ref, cn_ref), wc_ref[...], preferred_element_type=F32))
    x_ext = jnp.concatenate([x_ref[...], xp_ref[...], xn_ref[...]], axis=0)
    x1 = x_ext + mod_ref[2:3, :] * _rms(y, gpost_ref[...])
    x1_ref[...] = x1[:T, :]
    h2 = _rms(x1, gpre_ref[...]) * (1.0 + mod_ref[4:5, :]) + mod_ref[3:4, :]
    _interleave_rows(h2[:T, :], tmp_ref, lhs_ref)
    lhs_ref[T:, :] = h2[T:, :].astype(BF16)
    lhs = lhs_ref[...]
    sub = lax.broadcasted_iota(jnp.int32, (SUBLANES, 2 * FF_CHUNK), 0)

    for j in range(wu_ref.shape[0]):
        ue = jnp.dot(lhs, wu_ref[j], preferred_element_type=F32)
        u = ue[:T, :]
        edge = ue[T:, :]
        prev_row = jnp.where(first, 0.0, edge[half - 1:half, :])
        next_row = jnp.where(last, 0.0, edge[half:half + 1, :])
        head = jnp.where(sub == 0, prev_row, pltpu.roll(u[T - SUBLANES:, :], 1, axis=0))
        tail = jnp.where(sub == SUBLANES - 1, next_row, pltpu.roll(u[:SUBLANES, :], SUBLANES - 1, axis=0))
        up = jnp.concatenate([head, u[:T - SUBLANES, :]], axis=0)
        un = jnp.concatenate([u[SUBLANES:, :], tail], axis=0)
        cw = cw_ref[j]
        c = cw[0:1, :] * up + cw[1:2, :] * u + cw[2:3, :] * un + cb_ref[j]
        val = c[:, :FF_CHUNK]
        gate = c[:, FF_CHUNK:]
        act = gate * _sigmoid(gate) * val
        act_ref[:, j * FF_CHUNK:(j + 1) * FF_CHUNK] = act.astype(BF16)
    y = jnp.dot(act_ref[...], wd_ref[...], preferred_element_type=F32)
    r = _rms(y, g_ref[...])
    nl = y_ref.shape[0]
    for c in range(nl):
        y_ref[c] = r[:, c * LANE:(c + 1) * LANE]
    gt = mod_ref[5:6, :]
    for k in range(G):
        src = pl.ds((SUBLANES * k % G) * SUBLANES + SUBLANES * k // G, SUBLANES, stride=SUBLANES)
        rows = slice(k * SUBLANES, (k + 1) * SUBLANES)
        yk = jnp.concatenate([y_ref[c, src, :] for c in range(nl)], axis=1)
        o_ref[rows, :] = x1_ref[rows, :] + gt * yk


def _layer_tail(a, bs, att, xc, mod, wa, wb, wc, gpost, gpre, wu, cw, cb, wd, g, n_ctx, lat_only):
    B, S, _ = xc.shape
    T = TOK_TILE
    nct = n_ctx // T
    tile0 = nct if lat_only else 0
    ntiles = S // T - tile0
    seq_starts = (0,) if lat_only else (0, nct)
    seq_ends = (ntiles - 1,) if lat_only else (nct - 1, ntiles - 1)
    const = lambda shape: pl.BlockSpec(shape, lambda b, i: (0,) * len(shape), pipeline_mode=pl.Buffered(1))

    def tok(width, rows_total, off, blk=T):
        per = T // blk
        nblk = rows_total // blk
        return [
            pl.BlockSpec((None, T, width), lambda b, i: (b, i + off, 0)),
            pl.BlockSpec((None, blk, width), lambda b, i: (b, jnp.maximum((i + off) * per - 1, 0), 0)),
            pl.BlockSpec((None, blk, width), lambda b, i: (b, jnp.minimum((i + off + 1) * per, nblk - 1), 0)),
        ]

    per = T // HALO_ROWS
    nblk = S // HALO_ROWS
    bs_specs = [
        pl.BlockSpec((T, SSM_WIDTH), lambda b, i: (i + tile0, b)),
        pl.BlockSpec((HALO_ROWS, SSM_WIDTH), lambda b, i: (jnp.maximum((i + tile0) * per - 1, 0), b)),
        pl.BlockSpec((HALO_ROWS, SSM_WIDTH), lambda b, i: (jnp.minimum((i + tile0 + 1) * per, nblk - 1), b)),
    ]
    att_off = 0 if lat_only else tile0
    nch = wu.shape[0]
    return pl.pallas_call(
        functools.partial(_tail_kernel, seq_starts=seq_starts, seq_ends=seq_ends),
        out_shape=jax.ShapeDtypeStruct((B, ntiles * T, D_MODEL), F32),
        grid=(B, ntiles),
        in_specs=(
            tok(POOL_WIDTH, S, tile0, HALO_ROWS) + bs_specs + tok(ATTN_WIDTH, att.shape[1], att_off, HALO_ROWS)
            + tok(D_MODEL, S, tile0, SUBLANES)
            + [pl.BlockSpec((None, None, 6, D_MODEL), lambda b, i: (b, jnp.where(i + tile0 >= nct, 1, 0), 0, 0)),
               const((POOL_WIDTH, D_MODEL)), const((SSM_WIDTH, D_MODEL)), const((ATTN_WIDTH, D_MODEL)),
               const((1, D_MODEL)), const((1, D_MODEL)),
               const((nch, D_MODEL, 2 * FF_CHUNK)),
               const((nch, 3, 2 * FF_CHUNK)),
               const((nch, 1, 2 * FF_CHUNK)),
               const((D_FF, D_MODEL)),
               const((1, D_MODEL))]),
        out_specs=pl.BlockSpec((None, T, D_MODEL), lambda b, i: (b, i, 0)),
        scratch_shapes=[pltpu.VMEM((T, D_FF), BF16),
                        pltpu.VMEM((D_MODEL // LANE, T, LANE), F32),
                        pltpu.VMEM((D_MODEL // LANE, T + SUBLANES * SUBLANES, LANE), F32),
                        pltpu.VMEM((T + HALO_ROWS, D_MODEL), BF16),
                        pltpu.VMEM((T, D_MODEL), F32)],
        compiler_params=_cparams(("parallel", "parallel")),
        name="layer_tail",
    )(a, a, a, bs, bs, bs, att, att, att, xc, xc, xc, mod, wa, wb, wc, gpost, gpre, wu, cw, cb, wd, g)


def _zoh_blocks(lam_re, lam_im, log_dt, b_re, b_im, c_re, c_im):
    lr = jnp.minimum(lam_re, -1e-4)
    li = lam_im
    dt = jnp.exp(log_dt)[..., None]
    mag = jnp.exp(lr * dt)
    lbr = mag * jnp.cos(li * dt)
    lbi = mag * jnp.sin(li * dt)
    den = lr * lr + li * li
    fr = ((lbr - 1.0) * lr + lbi * li) / den
    fi = (lbi * lr - (lbr - 1.0) * li) / den
    bbr = fr[..., None] * b_re - fi[..., None] * b_im
    bbi = fr[..., None] * b_im + fi[..., None] * b_re
    eye = jnp.eye(SSM_GROUPS, dtype=F32)
    blk_b = lambda t: jnp.einsum('dgph,gk->dghkp', t, eye).reshape(2, SSM_WIDTH, SSM_LANES)
    blk_c = lambda t: jnp.einsum('dghp,gk->dgpkh', t, eye).reshape(2, SSM_LANES, SSM_WIDTH)
    bcat = jnp.concatenate([blk_b(bbr), blk_b(bbi)], axis=-1).astype(BF16)
    ccat = jnp.concatenate([blk_c(c_re), -blk_c(c_im)], axis=1).astype(BF16)
    lam = jnp.concatenate([lbr.reshape(2, 1, SSM_LANES), lbi.reshape(2, 1, SSM_LANES)], axis=-1)
    return bcat, lam, ccat


def _head_perm():
    cols = []
    for p in range(Q_PER_KV):
        for h in (p, Q_PER_KV + p):
            cols.extend(range(h * HEAD_DIM, (h + 1) * HEAD_DIM))
    return jnp.array(cols, dtype=jnp.int32)


def _rope_tables(n_ctx, n_lat):
    t = jnp.arange(n_lat)
    lane = jnp.arange(LANE)
    hl = lane % HEAD_DIM
    quarter = HEAD_DIM // 4
    inv_freq = ROPE_BASE ** (-jnp.arange(quarter, dtype=F32) / quarter)
    pos = jnp.where(hl[None, :] < HEAD_DIM // 2, (t // GRID_W)[:, None], (t % GRID_W)[:, None])
    ang = pos.astype(F32) * inv_freq[hl % quarter][None, :]
    first_half = (hl % (2 * quarter)) < quarter
    cos_l = jnp.cos(ang)
    sin_l = jnp.where(first_half[None, :], -jnp.sin(ang), jnp.sin(ang))
    cos_t = jnp.concatenate([jnp.ones((n_ctx, LANE), F32), cos_l], axis=0)
    sin_t = jnp.concatenate([jnp.zeros((n_ctx, LANE), F32), sin_l], axis=0)
    return cos_t, sin_t


def kernel(x, c, ctx, c_ctx, w_ada, b_ada, g_pre_mix, g_post_mix, g_pre_ffn, g_post_ffn, w_in, pool_w, pool_scale, lam_re, lam_im, log_dt, b_re, b_im, c_re, c_im, s5_d, w_glu, b_glu, sink, w_out, w_up, conv_w, conv_b, w_down):
    B, L, _ = x.shape
    N = ctx.shape[1]
    S = N + L
    assert N % TOK_TILE == 0 and L % TOK_TILE == 0 and L % GRID_W == 0

    rows = -(-(B + 1) // 8) * 8
    cond = jnp.concatenate([c, c_ctx[None, :], jnp.zeros((rows - B - 1, D_MODEL), F32)], axis=0)
    mod_all = _ada(cond, w_ada, b_ada)
    mod_lat = mod_all[:, :B].reshape(DEPTH, B, 1, 6, D_MODEL)
    mod_ctx = jnp.broadcast_to(mod_all[:, B].reshape(DEPTH, 1, 1, 6, D_MODEL), (DEPTH, B, 1, 6, D_MODEL))
    mods = jnp.concatenate([mod_ctx, mod_lat], axis=2)

    cos_t, sin_t = _rope_tables(N, L)
    perm = _head_perm()
    q0 = POOL_WIDTH + SSM_WIDTH
    in_cols = jnp.concatenate([jnp.arange(q0), q0 + perm, jnp.arange(q0 + ATTN_WIDTH, D_IN)])
    out_rows = jnp.concatenate([jnp.arange(q0), q0 + perm])
    eye_w = jnp.eye(len(POOL_WINDOWS), dtype=F32)
    nch = D_FF // FF_CHUNK

    def ffn_chunks(t):
        r = t.shape[0]
        return t.reshape(r, 2, nch, FF_CHUNK).transpose(2, 0, 1, 3).reshape(nch, r, 2 * FF_CHUNK)

    xc = jnp.concatenate([ctx, x], axis=1)
    for l in range(DEPTH):
        last = l == DEPTH - 1
        w_in_l = w_in[l][:, in_cols].astype(BF16)
        w_out_l = w_out[l][out_rows].astype(BF16)
        pool_blk = jnp.einsum('gcd,gk->gckd', pool_w[l], eye_w).reshape(POOL_WIDTH, POOL_WIDTH).astype(BF16)
        bcat, lam, ccat = _zoh_blocks(lam_re[l], lam_im[l], log_dt[l], b_re[l], b_im[l], c_re[l], c_im[l])
        sink_b = jnp.broadcast_to(sink[l][:, None] * LOG2_E, (N_Q_HEADS, LANE))
        wu = ffn_chunks(w_up[l]).astype(BF16)
        cw = ffn_chunks(conv_w[l])
        cb = ffn_chunks(conv_b[l][None, :])
        wd = w_down[l].astype(BF16)

        pool_in, us, q, kt, v = _in_proj(xc, mods[l], g_pre_mix[l][None, :], w_in_l, cos_t, sin_t, N // TOK_TILE)
        a = _pool(pool_in, pool_blk, pool_scale[l][None, :], N)
        bs = _s5(us.reshape(S * B, SSM_WIDTH), bcat, lam, ccat, s5_d[l][None, :],
                 w_glu[l].astype(BF16), b_glu[l][None, :], B, N)
        att = _attention(q, kt, v, sink_b, N, last)
        xc = _layer_tail(a, bs.reshape(S, B * SSM_WIDTH), att, xc, mods[l],
                         w_out_l[:POOL_WIDTH], w_out_l[POOL_WIDTH:q0], w_out_l[q0:],
                         g_post_mix[l][None, :], g_pre_ffn[l][None, :],
                         wu, cw, cb, wd, g_post_ffn[l][None, :], N, last)
    return xc
```

```python
import functools
import math

import jax
import jax.numpy as jnp
from jax import lax
from jax.experimental import pallas as pl
from jax.experimental.pallas import tpu as pltpu

F32 = jnp.float32
BF16 = jnp.bfloat16

D_MODEL = 1024
DEPTH = 4
GRID_W = 64
EPS = 1e-6
POOL_WINDOWS = (2, 4, 8, 16)
POOL_WIDTH = 256
POOL_GROUP = 64
SSM_WIDTH = 256
SSM_CH = 16
SSM_GROUPS = 16
SSM_STATE = 64
SSM_LANES = SSM_GROUPS * SSM_STATE
HEAD_DIM = 64
ATTN_WIDTH = 512
N_Q_HEADS = 8
N_KV_HEADS = 2
Q_PER_KV = 4
KV_WIDTH = 128
WINDOW = 128
BLOCK = 128
ROPE_BASE = 10000.0
D_FF = 2816
D_IN = 1280
NEG_INF = -1e30
LOG2_E = 1.4426950408889634

LANE = 128
POOL_PAD = 16
POOL_HALO = 8
POOL_CHUNK = 128
TOK_TILE = 256
S5_STEPS = 16
FF_CHUNK = 256
HALO_ROWS = 16
SUBLANES = 8
IN_SPLIT = 2
ATTN_UNIT_PAIRS = ((0, 1, 2, 3),)
VMEM_LIMIT = 56 * 1024 * 1024


def _cparams(sem):
    return pltpu.CompilerParams(dimension_semantics=sem, vmem_limit_bytes=VMEM_LIMIT)


def _rms(x, g):
    return x * lax.rsqrt(jnp.mean(x * x, axis=-1, keepdims=True) + EPS) * g


def _sigmoid(x):
    return 1.0 / (1.0 + jnp.exp(-x))


def _gelu_tanh(x):
    c = math.sqrt(2.0 / math.pi)
    return 0.5 * x * (1.0 + jnp.tanh(c * (x + 0.044715 * (x * x * x))))


def _ada_kernel(c_ref, w_ref, b_ref, o_ref):
    c = c_ref[...]
    s = c * _sigmoid(c)
    o_ref[...] = jnp.dot(s.astype(BF16), w_ref[...].astype(BF16), preferred_element_type=F32) + b_ref[...]


def _ada(cond, w_ada, b_ada):
    rows = cond.shape[0]
    nblk = w_ada.shape[-1] // D_MODEL
    return pl.pallas_call(
        _ada_kernel,
        out_shape=jax.ShapeDtypeStruct((DEPTH, rows, nblk * D_MODEL), F32),
        grid=(DEPTH, nblk),
        in_specs=[
            pl.BlockSpec((rows, D_MODEL), lambda l, j: (0, 0)),
            pl.BlockSpec((None, D_MODEL, D_MODEL), lambda l, j: (l, 0, j)),
            pl.BlockSpec((None, 1, D_MODEL), lambda l, j: (l, 0, j)),
        ],
        out_specs=pl.BlockSpec((None, rows, D_MODEL), lambda l, j: (l, 0, j)),
        compiler_params=_cparams(("arbitrary", "arbitrary")),
        name="ada_mod",
    )(cond, w_ada, b_ada.reshape(DEPTH, 1, -1))


def _in_proj_kernel(*refs, nct, dual):
    x_refs, rest = refs[:2 if dual else 1], refs[2 if dual else 1:]
    mod_ref, g_ref, w_ref, cos_ref, sin_ref, pool_ref, us_ref, q_ref, kt_ref, v_ref = rest
    is_ctx = pl.program_id(1) < nct
    T = x_refs[0].shape[0]
    lane = lax.broadcasted_iota(jnp.int32, (T // IN_SPLIT, LANE), 1)
    first_half = (lane % 32) < 16
    q0 = POOL_WIDTH + SSM_WIDTH
    k0 = q0 + ATTN_WIDTH
    scale = HEAD_DIM ** -0.5 * LOG2_E
    w = w_ref[...]
    for part in range(IN_SPLIT):
        rows = slice(part * (T // IN_SPLIT), (part + 1) * (T // IN_SPLIT))
        x = x_refs[0][rows, :]
        if dual:
            x = jnp.where(is_ctx, x, x_refs[1][rows, :])
        h = _rms(x, g_ref[...]) * (1.0 + mod_ref[1:2, :]) + mod_ref[0:1, :]
        z = jnp.dot(h.astype(BF16), w, preferred_element_type=F32)
        pool_ref[rows, :] = z[:, 0:POOL_WIDTH]
        us_ref[rows, :] = z[:, POOL_WIDTH:POOL_WIDTH + SSM_WIDTH]
        cos = cos_ref[rows, :]
        sin = sin_ref[rows, :]

        def rope(t):
            partner = jnp.where(first_half, pltpu.roll(t, LANE - 16, axis=1), pltpu.roll(t, 16, axis=1))
            return t * cos + partner * sin

        for j in range(ATTN_WIDTH // LANE):
            t = z[:, q0 + j * LANE:q0 + (j + 1) * LANE]
            q_ref[rows, j * LANE:(j + 1) * LANE] = (rope(t) * scale).astype(BF16)
        kt_ref[:, rows] = rope(z[:, k0:k0 + KV_WIDTH]).T.astype(BF16)
        v_ref[rows, :] = z[:, k0 + KV_WIDTH:k0 + 2 * KV_WIDTH].astype(BF16)


def _stream_views(ctx_src, lat_src, nct):
    if lat_src is None:
        return (ctx_src,), (lambda i: i,)
    return (ctx_src, lat_src), (lambda i: jnp.minimum(i, nct - 1), lambda i: jnp.maximum(i - nct, 0))


def _in_proj(ctx_src, lat_src, S, mod, g, w_in, cos_t, sin_t, nct):
    B = ctx_src.shape[0]
    T = TOK_TILE
    grid = (B, S // T)
    tok = lambda width: pl.BlockSpec((None, T, width), lambda b, i: (b, i, 0))
    srcs, views = _stream_views(ctx_src, lat_src, nct)
    x_specs = [pl.BlockSpec((None, T, D_MODEL), lambda b, i, view=view: (b, view(i), 0)) for view in views]
    return pl.pallas_call(
        functools.partial(_in_proj_kernel, nct=nct, dual=len(srcs) == 2),
        out_shape=(
            jax.ShapeDtypeStruct((B, S, POOL_WIDTH), F32),
            jax.ShapeDtypeStruct((S, B * SSM_WIDTH), F32),
            jax.ShapeDtypeStruct((B, S, ATTN_WIDTH), BF16),
            jax.ShapeDtypeStruct((B, KV_WIDTH, S), BF16),
            jax.ShapeDtypeStruct((B, S, KV_WIDTH), BF16),
        ),
        grid=grid,
        in_specs=x_specs + [
            pl.BlockSpec((None, None, 6, D_MODEL), lambda b, i: (b, jnp.where(i >= nct, 1, 0), 0, 0)),
            pl.BlockSpec((1, D_MODEL), lambda b, i: (0, 0)),
            pl.BlockSpec((D_MODEL, D_IN), lambda b, i: (0, 0)),
            pl.BlockSpec((T, LANE), lambda b, i: (i, 0)),
            pl.BlockSpec((T, LANE), lambda b, i: (i, 0)),
        ],
        out_specs=(
            tok(POOL_WIDTH),
            pl.BlockSpec((T, SSM_WIDTH), lambda b, i: (i, b)),
            tok(ATTN_WIDTH),
            pl.BlockSpec((None, KV_WIDTH, T), lambda b, i: (b, 0, i)),
            tok(KV_WIDTH),
        ),
        compiler_params=_cparams(("parallel", "parallel")),
        name="in_proj",
    )(*srcs, mod, g, w_in, cos_t, sin_t)


def _pool_kernel(u_ref, w_ref, sc_ref, o_ref, pad_ref, *, n_ctx, n_lat):
    C = POOL_CHUNK
    zeros = jnp.zeros((POOL_PAD, POOL_WIDTH), F32)
    ctx0 = POOL_PAD
    lat0 = 2 * POOL_PAD + n_ctx
    pad_ref[0:POOL_PAD, :] = zeros
    pad_ref[ctx0 + n_ctx:lat0, :] = zeros
    pad_ref[lat0 + n_lat:lat0 + n_lat + POOL_PAD, :] = zeros
    pad_ref[ctx0:ctx0 + n_ctx, :] = u_ref[0:n_ctx, :]
    pad_ref[lat0:lat0 + n_lat, :] = u_ref[n_ctx:n_ctx + n_lat, :]

    nw = len(POOL_WINDOWS)
    t_idx = lax.broadcasted_iota(jnp.int32, (C, C + 2 * POOL_HALO), 0)
    s_idx = lax.broadcasted_iota(jnp.int32, (C, C + 2 * POOL_HALO), 1) - POOL_HALO
    band = jnp.concatenate(
        [jnp.where((s_idx >= t_idx - w // 2) & (s_idx < t_idx - w // 2 + w), 1.0, 0.0).astype(BF16)
         for w in POOL_WINDOWS], axis=0)
    grp = lax.broadcasted_iota(jnp.int32, (C, POOL_WIDTH), 1) // POOL_GROUP
    row = lax.broadcasted_iota(jnp.int32, (C, POOL_WIDTH), 0)
    w_lane = functools.reduce(lambda acc, gw: jnp.where(grp == gw[0], gw[1], acc),
                              list(enumerate(POOL_WINDOWS))[1:], jnp.full((C, POOL_WIDTH), POOL_WINDOWS[0]))
    inv_w = 1.0 / w_lane.astype(F32)
    wmat = w_ref[...]
    scale = sc_ref[...]

    def window_sums(c, pad_base):
        base = pad_base + c * C
        win = pad_ref[base - POOL_HALO:base + C + POOL_HALO, :]
        hi = win.astype(BF16)
        lo = (win - hi.astype(F32)).astype(BF16)
        return (jnp.dot(band, hi, preferred_element_type=F32)
                + jnp.dot(band, lo, preferred_element_type=F32))

    def finish(sums, c, pad_base, out_base, seq_len):
        base = pad_base + c * C
        pooled = sums[0:C, :]
        for gi in range(1, nw):
            pooled = jnp.where(grp == gi, sums[gi * C:(gi + 1) * C, :], pooled)
        if c * C >= POOL_HALO and (c + 1) * C + POOL_HALO <= seq_len:
            pooled = pooled * inv_w
        else:
            start = row + c * C - w_lane // 2
            cnt = jnp.minimum(start + w_lane, seq_len) - jnp.maximum(start, 0)
            pooled = pooled / cnt.astype(F32)
        pooled = pooled - pad_ref[base:base + C, :]
        mixed = jnp.dot(pooled.astype(BF16), wmat, preferred_element_type=F32) * scale
        o_ref[out_base + c * C:out_base + (c + 1) * C, :] = mixed.astype(BF16)

    jobs = ([(c, ctx0, 0, n_ctx) for c in range(n_ctx // C)]
            + [(c, lat0, n_ctx, n_lat) for c in range(n_lat // C)])
    sums = window_sums(jobs[0][0], jobs[0][1])
    for k, job in enumerate(jobs):
        nxt = window_sums(jobs[k + 1][0], jobs[k + 1][1]) if k + 1 < len(jobs) else None
        finish(sums, *job)
        sums = nxt


def _pool(pool_in, w_blk, scale, n_ctx):
    B, S, _ = pool_in.shape
    n_lat = S - n_ctx
    return pl.pallas_call(
        functools.partial(_pool_kernel, n_ctx=n_ctx, n_lat=n_lat),
        out_shape=jax.ShapeDtypeStruct((B, S, POOL_WIDTH), BF16),
        grid=(B,),
        in_specs=[
            pl.BlockSpec((None, S, POOL_WIDTH), lambda b: (b, 0, 0)),
            pl.BlockSpec((POOL_WIDTH, POOL_WIDTH), lambda b: (0, 0)),
            pl.BlockSpec((1, POOL_WIDTH), lambda b: (0, 0)),
        ],
        out_specs=pl.BlockSpec((None, S, POOL_WIDTH), lambda b: (b, 0, 0)),
        scratch_shapes=[pltpu.VMEM((S + 3 * POOL_PAD, POOL_WIDTH), F32)],
        compiler_params=_cparams(("parallel",)),
        name="pool_mixer",
    )(pool_in, w_blk, scale)


def _s5_scan(lam_ref, bu_ref, hb_ref, state, *, batch, reverse):
    for cb in range(SSM_LANES // LANE):
        re = slice(cb * LANE, (cb + 1) * LANE)
        im = slice(SSM_LANES + cb * LANE, SSM_LANES + (cb + 1) * LANE)
        lr = jnp.broadcast_to(lam_ref[0:1, re], (batch, LANE))
        li = jnp.broadcast_to(lam_ref[0:1, im], (batch, LANE))
        s_re = state[:, re]
        s_im = state[:, im]
        for step in range(S5_STEPS):
            t = S5_STEPS - 1 - step if reverse else step
            rows = slice(t * batch, (t + 1) * batch)
            n_re = lr * s_re - li * s_im + bu_ref[rows, re]
            n_im = lr * s_im + li * s_re + bu_ref[rows, im]
            hb_ref[rows, re] = n_re.astype(BF16)
            hb_ref[rows, im] = n_im.astype(BF16)
            s_re, s_im = n_re, n_im
        state[:, re] = s_re
        state[:, im] = s_im


def _s5_pair(u_ref, bcat_ref, lam_ref, ccat_ref, bu0, bu1, hb0, hb1, state, *, batch, reverse):
    R = S5_STEPS * batch

    @pl.when(pl.program_id(0) == 0)
    def _():
        state[...] = jnp.zeros(state.shape, F32)
        for ref in (bu0, bu1, hb0, hb1):
            ref[...] = jnp.zeros(ref.shape, ref.dtype)

    first, second = (slice(R, 2 * R), slice(0, R)) if reverse else (slice(0, R), slice(R, 2 * R))
    bcat = bcat_ref[...]
    ccat = ccat_ref[...]
    y_a = jnp.dot(hb0[...], ccat, preferred_element_type=F32)
    bu0[...] = jnp.dot(u_ref[first, :].astype(BF16), bcat, preferred_element_type=F32)
    _s5_scan(lam_ref, bu1, hb1, state, batch=batch, reverse=reverse)
    bu1[...] = jnp.dot(u_ref[second, :].astype(BF16), bcat, preferred_element_type=F32)
    y_b = jnp.dot(hb1[...], ccat, preferred_element_type=F32)
    _s5_scan(lam_ref, bu0, hb0, state, batch=batch, reverse=reverse)
    return jnp.concatenate([y_b, y_a] if reverse else [y_a, y_b], axis=0)


def _s5_fwd_kernel(u_ref, bcat_ref, lam_ref, ccat_ref, y_ref, bu0, bu1, hb0, hb1, state, *, batch):
    y_ref[...] = _s5_pair(u_ref, bcat_ref, lam_ref, ccat_ref, bu0, bu1, hb0, hb1, state,
                          batch=batch, reverse=False)


def _s5_bwd_kernel(u_ref, bcat_ref, lam_ref, ccat_ref, u2_ref, yf_ref, dsk_ref, wglu_ref, bglu_ref,
                   o_ref, bu0, bu1, hb0, hb1, state, *, batch):
    y = _s5_pair(u_ref, bcat_ref, lam_ref, ccat_ref, bu0, bu1, hb0, hb1, state, batch=batch, reverse=True)
    tot = yf_ref[...] + y + dsk_ref[...] * u2_ref[...]
    g = _gelu_tanh(tot)
    gate = jnp.dot(g.astype(BF16), wglu_ref[...], preferred_element_type=F32) + bglu_ref[...]
    o_ref[...] = (g * _sigmoid(gate)).astype(BF16)


def _s5(us_tm, bcat, lam, ccat, dskip, w_glu, b_glu, batch, n_ctx):
    rows_total = us_tm.shape[0]
    R = S5_STEPS * batch
    nch = rows_total // (2 * R)
    nctx = n_ctx // (2 * S5_STEPS)
    assert rows_total % (2 * R) == 0 and n_ctx % (2 * S5_STEPS) == 0
    const = lambda shape: pl.BlockSpec(shape, lambda j: (0,) * len(shape))

    def dir_spec(d, shape):
        return pl.BlockSpec((None,) + shape, lambda j: (d,) + (0,) * len(shape))

    def chunk_spec(order, lag):
        return pl.BlockSpec((2 * R, SSM_WIDTH), lambda j: (order(jnp.clip(j - lag, 0, nch - 1)), 0))

    fwd = lambda c: c
    rev = lambda c: jnp.where(c < nctx, nctx - 1 - c, nch + nctx - 1 - c)

    scratch = [pltpu.VMEM((R, 2 * SSM_LANES), F32), pltpu.VMEM((R, 2 * SSM_LANES), F32),
               pltpu.VMEM((R, 2 * SSM_LANES), BF16), pltpu.VMEM((R, 2 * SSM_LANES), BF16),
               pltpu.VMEM((batch, 2 * SSM_LANES), F32)]
    yf = pl.pallas_call(
        functools.partial(_s5_fwd_kernel, batch=batch),
        out_shape=jax.ShapeDtypeStruct((rows_total, SSM_WIDTH), F32),
        grid=(nch + 1,),
        in_specs=[
            chunk_spec(fwd, 0),
            dir_spec(0, (SSM_WIDTH, 2 * SSM_LANES)),
            dir_spec(0, (1, 2 * SSM_LANES)),
            dir_spec(0, (2 * SSM_LANES, SSM_WIDTH)),
        ],
        out_specs=chunk_spec(fwd, 1),
        scratch_shapes=scratch,
        compiler_params=_cparams(("arbitrary",)),
        name="s5_forward",
    )(us_tm, bcat, lam, ccat)

    return pl.pallas_call(
        functools.partial(_s5_bwd_kernel, batch=batch),
        out_shape=jax.ShapeDtypeStruct((rows_total, SSM_WIDTH), BF16),
        grid=(nch + 1,),
        in_specs=[
            chunk_spec(rev, 0),
            dir_spec(1, (SSM_WIDTH, 2 * SSM_LANES)),
            dir_spec(1, (1, 2 * SSM_LANES)),
            dir_spec(1, (2 * SSM_LANES, SSM_WIDTH)),
            chunk_spec(rev, 1),
            chunk_spec(rev, 1),
            const((1, SSM_WIDTH)),
            const((SSM_WIDTH, SSM_WIDTH)),
            const((1, SSM_WIDTH)),
        ],
        out_specs=chunk_spec(rev, 1),
        scratch_shapes=scratch,
        compiler_params=_cparams(("arbitrary",)),
        name="s5_backward",
    )(us_tm, bcat, lam, ccat, us_tm, yf, dskip, w_glu, b_glu)


def _attn_kernel(q_ref, kc_ref, vc_ref, kp_ref, kq_ref, kn_ref, vp_ref, vq_ref, vn_ref, sink_ref,
                 o_ref, s_ref, e_ref, *, nct, n_lat, tile0):
    i = pl.program_id(1) + tile0
    low = lax.broadcasted_iota(jnp.int32, (BLOCK, LANE), 1) < HEAD_DIM

    def run(kcat, vcat, biases):
        low_v = lax.broadcasted_iota(jnp.int32, vcat.shape, 1) < HEAD_DIM
        one = jnp.ones_like(vcat)
        vones = (jnp.where(low_v, vcat, one), jnp.where(low_v, one, vcat))
        npair = ATTN_WIDTH // LANE
        nk = kcat.shape[1]
        qs = [q_ref[:, p * LANE:(p + 1) * LANE] for p in range(npair)]
        zero = jnp.zeros_like(qs[0])
        units = [(half, pairs) for pairs in ATTN_UNIT_PAIRS for half in (0, 1)]
        per_unit = len(ATTN_UNIT_PAIRS[0])
        stacked = [None if b is None else jnp.concatenate([b] * per_unit, axis=0) for b in biases]

        def scores(u):
            half, pairs = units[u]
            q = jnp.concatenate([jnp.where(low, qs[p], zero) if half == 0 else jnp.where(low, zero, qs[p])
                                 for p in pairs], axis=0)
            s_ref[u, :, :nk] = jnp.dot(q, kcat, preferred_element_type=F32)

        def softmax(u):
            half, pairs = units[u]
            sink = jnp.concatenate(
                [jnp.broadcast_to(sink_ref[half * Q_PER_KV + p:half * Q_PER_KV + p + 1, 0:1], (BLOCK, 1))
                 for p in pairs], axis=0)
            tiles = []
            for t, bias in enumerate(stacked):
                st = s_ref[u, :, t * LANE:(t + 1) * LANE]
                tiles.append(st if bias is None else st + bias)
            mx = functools.reduce(jnp.maximum, tiles)
            m = jnp.maximum(jnp.max(mx, axis=-1, keepdims=True), sink)
            for t, st in enumerate(tiles):
                e_ref[u, :, t * LANE:(t + 1) * LANE] = jnp.exp2(st - m).astype(BF16)
            return jnp.exp2(sink - m)

        def values(u):
            return jnp.dot(e_ref[u, :, :nk], vones[units[u][0]], preferred_element_type=F32)

        nu = len(units)
        sink_terms, accs = [None] * nu, [None] * nu
        scores(0)
        scores(1)
        for u in range(nu):
            sink_terms[u] = softmax(u)
            if u + 2 < nu:
                scores(u + 2)
            accs[u] = values(u)
        low2 = jnp.concatenate([low] * per_unit, axis=0)
        for g in range(nu // 2):
            a_lo, a_hi = accs[2 * g], accs[2 * g + 1]
            num = jnp.where(low2, a_lo, a_hi)
            den = (jnp.where(low2, pltpu.roll(a_lo, HEAD_DIM, axis=1), pltpu.roll(a_hi, HEAD_DIM, axis=1))
                   + jnp.where(low2, sink_terms[2 * g], sink_terms[2 * g + 1]))
            out = (num / den).astype(BF16)
            for r, p in enumerate(units[2 * g][1]):
                o_ref[:, p * LANE:(p + 1) * LANE] = out[r * BLOCK:(r + 1) * BLOCK, :]

    @pl.when(i < nct)
    def _():
        run(kc_ref[...], vc_ref[...], [None] * (kc_ref.shape[1] // LANE))

    @pl.when(i >= nct)
    def _():
        li = i - nct
        r = lax.broadcasted_iota(jnp.int32, (BLOCK, BLOCK), 0)
        c = lax.broadcasted_iota(jnp.int32, (BLOCK, BLOCK), 1)
        bias_p = jnp.where((c >= r) & (li >= 1), 0.0, NEG_INF)
        bias_n = jnp.where((c <= r) & ((li + 2) * BLOCK <= n_lat), 0.0, NEG_INF)
        kcat = jnp.concatenate([kc_ref[...], kp_ref[...], kq_ref[...], kn_ref[...]], axis=1)
        vcat = jnp.concatenate([vc_ref[...], vp_ref[...], vq_ref[...], vn_ref[...]], axis=0)
        run(kcat, vcat, [None] * (kc_ref.shape[1] // LANE) + [bias_p, None, bias_n])


def _attention(q, kt, v, sink_b, n_ctx, lat_only):
    B, S, _ = q.shape
    nct = n_ctx // BLOCK
    ntiles = S // BLOCK
    tile0 = nct if lat_only else 0
    rows_out = S - tile0 * BLOCK
    nunit = 2 * len(ATTN_UNIT_PAIRS)
    unit_rows = len(ATTN_UNIT_PAIRS[0]) * BLOCK
    cur = lambda i: i + tile0
    prev = lambda i: jnp.maximum(i + tile0 - 1, nct)
    nxt = lambda i: jnp.minimum(i + tile0 + 1, ntiles - 1)
    vblk = lambda f: pl.BlockSpec((None, BLOCK, KV_WIDTH), lambda b, i: (b, f(i), 0))
    kblk = lambda f: pl.BlockSpec((None, KV_WIDTH, BLOCK), lambda b, i: (b, 0, f(i)))
    return pl.pallas_call(
        functools.partial(_attn_kernel, nct=nct, n_lat=S - n_ctx, tile0=tile0),
        out_shape=jax.ShapeDtypeStruct((B, rows_out, ATTN_WIDTH), BF16),
        grid=(B, ntiles - tile0),
        in_specs=[
            pl.BlockSpec((None, BLOCK, ATTN_WIDTH), lambda b, i: (b, cur(i), 0)),
            pl.BlockSpec((None, KV_WIDTH, n_ctx), lambda b, i: (b, 0, 0)),
            pl.BlockSpec((None, n_ctx, KV_WIDTH), lambda b, i: (b, 0, 0)),
            kblk(prev), kblk(cur), kblk(nxt),
            vblk(prev), vblk(cur), vblk(nxt),
            pl.BlockSpec((N_Q_HEADS, LANE), lambda b, i: (0, 0)),
        ],
        out_specs=pl.BlockSpec((None, BLOCK, ATTN_WIDTH), lambda b, i: (b, i, 0)),
        scratch_shapes=[pltpu.VMEM((nunit, unit_rows, n_ctx + 3 * BLOCK), F32),
                        pltpu.VMEM((nunit, unit_rows, n_ctx + 3 * BLOCK), BF16)],
        compiler_params=_cparams(("parallel", "parallel")),
        name="attention",
    )(q, kt, v, kt, kt, kt, v, v, v, sink_b)


def _interleave_rows(h2, tmp_ref, dst_ref):
    nl = tmp_ref.shape[0]
    T = h2.shape[0]
    G = T // SUBLANES
    pitch = G + SUBLANES
    for c in range(nl):
        for s in range(SUBLANES):
            tmp_ref[c, s * pitch:s * pitch + G, :] = h2[s * G:(s + 1) * G, c * LANE:(c + 1) * LANE]
    pack = 2
    for v in range(0, G, pack):
        blk = jnp.concatenate(
            [jnp.concatenate([tmp_ref[c, pl.ds(v + d, SUBLANES, stride=pitch), :] for c in range(nl)], axis=1)
             for d in range(pack)], axis=0)
        dst_ref[v * SUBLANES:(v + pack) * SUBLANES, :] = blk.astype(BF16)


def _tail_kernel(a_ref, ap_ref, an_ref, b_ref, bp_ref, bn_ref, c_ref, cp_ref, cn_ref, *refs,
                 ntiles, ctx_tiles, dual, seq_starts, seq_ends):
    n = pl.program_id(0)
    nx = 6 if dual else 3
    x_refs, rest = refs[:nx], refs[nx:]
    (mod_ref, modp_ref, wa_ref, wb_ref, wc_ref, gpost_ref, gpre_ref, wu_ref, cw_ref, cb_ref, wd_ref, g_ref,
     o_ref, act_ref, y_ref, tmp_ref, lhs_nxt, x1_nxt, lhs_ref, x1_ref, yp_ref) = rest
    nsteps = pl.num_programs(0)
    T = x_refs[0].shape[0]
    G = T // SUBLANES
    half = HALO_ROWS // 2

    @pl.when(n == 0)
    def _():
        lhs_ref[...] = jnp.zeros(lhs_ref.shape, BF16)
        x1_ref[...] = jnp.zeros(x1_ref.shape, F32)

    i = lax.rem(jnp.maximum(n - 1, 0), ntiles)
    first = functools.reduce(jnp.logical_or, [i == s for s in seq_starts])
    last = functools.reduce(jnp.logical_or, [i == e for e in seq_ends])
    lhs = lhs_ref[...]
    sub = lax.broadcasted_iota(jnp.int32, (SUBLANES, 2 * FF_CHUNK), 0)

    for j in range(wu_ref.shape[0]):
        ue = jnp.dot(lhs, wu_ref[j], preferred_element_type=F32)
        u = ue[:T, :]
        edge = ue[T:, :]
        prev_row = jnp.where(first, 0.0, edge[half - 1:half, :])
        next_row = jnp.where(last, 0.0, edge[half:half + 1, :])
        head = jnp.where(sub == 0, prev_row, pltpu.roll(u[T - SUBLANES:, :], 1, axis=0))
        tail = jnp.where(sub == SUBLANES - 1, next_row, pltpu.roll(u[:SUBLANES, :], SUBLANES - 1, axis=0))
        up = jnp.concatenate([head, u[:T - SUBLANES, :]], axis=0)
        un = jnp.concatenate([u[SUBLANES:, :], tail], axis=0)
        cw = cw_ref[j]
        c = cw[0:1, :] * up + cw[1:2, :] * u + cw[2:3, :] * un + cb_ref[j]
        val = c[:, :FF_CHUNK]
        gate = c[:, FF_CHUNK:]
        act = gate * _sigmoid(gate) * val
        act_ref[:, j * FF_CHUNK:(j + 1) * FF_CHUNK] = act.astype(BF16)

    def ext(ref, prev_ref, next_ref):
        return jnp.concatenate([ref[...], prev_ref[half:, :], next_ref[:half, :]], axis=0)

    yp_ref[...] = (jnp.dot(ext(a_ref, ap_ref, an_ref), wa_ref[...], preferred_element_type=F32)
                   + jnp.dot(ext(b_ref, bp_ref, bn_ref), wb_ref[...], preferred_element_type=F32)
                   + jnp.dot(ext(c_ref, cp_ref, cn_ref), wc_ref[...], preferred_element_type=F32))
    y = jnp.dot(act_ref[...], wd_ref[...], preferred_element_type=F32)

    x_ext = jnp.concatenate([r[...] for r in x_refs[:3]], axis=0)
    if dual:
        is_ctx = lax.rem(jnp.minimum(n, nsteps - 2), ntiles) < ctx_tiles
        x_ext = jnp.where(is_ctx, x_ext, jnp.concatenate([r[...] for r in x_refs[3:]], axis=0))
    x1 = x_ext + mod_ref[2:3, :] * _rms(yp_ref[...], gpost_ref[...])
    x1_nxt[...] = x1[:T, :]
    h2 = _rms(x1, gpre_ref[...]) * (1.0 + mod_ref[4:5, :]) + mod_ref[3:4, :]
    _interleave_rows(h2[:T, :], tmp_ref, lhs_nxt)
    lhs_nxt[T:, :] = h2[T:, :].astype(BF16)

    r = _rms(y, g_ref[...])
    nl = y_ref.shape[0]
    for c in range(nl):
        y_ref[c] = r[:, c * LANE:(c + 1) * LANE]
    gt = modp_ref[5:6, :]
    for k in range(G):
        src = pl.ds((SUBLANES * k % G) * SUBLANES + SUBLANES * k // G, SUBLANES, stride=SUBLANES)
        rows = slice(k * SUBLANES, (k + 1) * SUBLANES)
        yk = jnp.concatenate([y_ref[c, src, :] for c in range(nl)], axis=1)
        o_ref[rows, :] = x1_ref[rows, :] + gt * yk
    lhs_ref[...] = lhs_nxt[...]
    x1_ref[...] = x1_nxt[...]


def _layer_tail(a, bs, att, ctx_src, lat_src, mod, wa, wb, wc, gpost, gpre, wu, cw, cb, wd, g, n_ctx, lat_only):
    B, S, _ = a.shape
    T = TOK_TILE
    nct = n_ctx // T
    tile0 = nct if lat_only else 0
    ntiles = S // T - tile0
    srcs, views = _stream_views(ctx_src, lat_src, nct)
    seq_starts = (0,) if lat_only else (0, nct)
    seq_ends = (ntiles - 1,) if lat_only else (nct - 1, ntiles - 1)
    const = lambda shape: pl.BlockSpec(shape, lambda n: (0,) * len(shape), pipeline_mode=pl.Buffered(1))
    ntot = B * ntiles

    def tile(n, lag=0):
        t = jnp.clip(n - lag, 0, ntot - 1)
        return t // ntiles, lax.rem(t, ntiles)

    def tok(width, rows_total, off, blk=T, view=lambda i: i):
        per = T // blk
        nblk = rows_total // blk

        def at(f):
            def index_map(n):
                b, i = tile(n)
                return b, f(view(i + off)), 0
            return index_map

        return [
            pl.BlockSpec((None, T, width), at(lambda i: i)),
            pl.BlockSpec((None, blk, width), at(lambda i: jnp.maximum(i * per - 1, 0))),
            pl.BlockSpec((None, blk, width), at(lambda i: jnp.minimum((i + 1) * per, nblk - 1))),
        ]

    per = T // HALO_ROWS
    nblk = S // HALO_ROWS

    def bs_at(f):
        def index_map(n):
            b, i = tile(n)
            return f(i + tile0), b
        return index_map

    bs_specs = [
        pl.BlockSpec((T, SSM_WIDTH), bs_at(lambda i: i)),
        pl.BlockSpec((HALO_ROWS, SSM_WIDTH), bs_at(lambda i: jnp.maximum(i * per - 1, 0))),
        pl.BlockSpec((HALO_ROWS, SSM_WIDTH), bs_at(lambda i: jnp.minimum((i + 1) * per, nblk - 1))),
    ]

    def mod_spec(lag):
        def index_map(n):
            b, i = tile(n, lag)
            return b, jnp.where(i + tile0 >= nct, 1, 0), 0, 0
        return pl.BlockSpec((None, None, 6, D_MODEL), index_map)

    def out_map(n):
        b, i = tile(n, 1)
        return b, i, 0

    att_off = 0 if lat_only else tile0
    nch = wu.shape[0]
    return pl.pallas_call(
        functools.partial(_tail_kernel, ntiles=ntiles, ctx_tiles=nct - tile0, dual=len(srcs) == 2,
                          seq_starts=seq_starts, seq_ends=seq_ends),
        out_shape=jax.ShapeDtypeStruct((B, ntiles * T, D_MODEL), F32),
        grid=(ntot + 1,),
        in_specs=(
            tok(POOL_WIDTH, S, tile0, HALO_ROWS) + bs_specs + tok(ATTN_WIDTH, att.shape[1], att_off, HALO_ROWS)
            + [spec for src, view in zip(srcs, views)
               for spec in tok(D_MODEL, src.shape[1], tile0, SUBLANES, view)]
            + [mod_spec(0), mod_spec(1),
               const((POOL_WIDTH, D_MODEL)), const((SSM_WIDTH, D_MODEL)), const((ATTN_WIDTH, D_MODEL)),
               const((1, D_MODEL)), const((1, D_MODEL)),
               const((nch, D_MODEL, 2 * FF_CHUNK)),
               const((nch, 3, 2 * FF_CHUNK)),
               const((nch, 1, 2 * FF_CHUNK)),
               const((D_FF, D_MODEL)),
               const((1, D_MODEL))]),
        out_specs=pl.BlockSpec((None, T, D_MODEL), out_map),
        scratch_shapes=[pltpu.VMEM((T, D_FF), BF16),
                        pltpu.VMEM((D_MODEL // LANE, T, LANE), F32),
                        pltpu.VMEM((D_MODEL // LANE, T + SUBLANES * SUBLANES, LANE), F32),
                        pltpu.VMEM((T + HALO_ROWS, D_MODEL), BF16),
                        pltpu.VMEM((T, D_MODEL), F32),
                        pltpu.VMEM((T + HALO_ROWS, D_MODEL), BF16),
                        pltpu.VMEM((T, D_MODEL), F32),
                        pltpu.VMEM((T + HALO_ROWS, D_MODEL), F32)],
        compiler_params=_cparams(("arbitrary",)),
        name="layer_tail",
    )(a, a, a, bs, bs, bs, att, att, att, *[src for src in srcs for _ in range(3)],
      mod, mod, wa, wb, wc, gpost, gpre, wu, cw, cb, wd, g)


def _zoh_blocks(lam_re, lam_im, log_dt, b_re, b_im, c_re, c_im):
    lr = jnp.minimum(lam_re, -1e-4)
    li = lam_im
    dt = jnp.exp(log_dt)[..., None]
    mag = jnp.exp(lr * dt)
    lbr = mag * jnp.cos(li * dt)
    lbi = mag * jnp.sin(li * dt)
    den = lr * lr + li * li
    fr = ((lbr - 1.0) * lr + lbi * li) / den
    fi = (lbi * lr - (lbr - 1.0) * li) / den
    bbr = fr[..., None] * b_re - fi[..., None] * b_im
    bbi = fr[..., None] * b_im + fi[..., None] * b_re
    eye = jnp.eye(SSM_GROUPS, dtype=F32)
    blk_b = lambda t: jnp.einsum('dgph,gk->dghkp', t, eye).reshape(2, SSM_WIDTH, SSM_LANES)
    blk_c = lambda t: jnp.einsum('dghp,gk->dgpkh', t, eye).reshape(2, SSM_LANES, SSM_WIDTH)
    bcat = jnp.concatenate([blk_b(bbr), blk_b(bbi)], axis=-1).astype(BF16)
    ccat = jnp.concatenate([blk_c(c_re), -blk_c(c_im)], axis=1).astype(BF16)
    lam = jnp.concatenate([lbr.reshape(2, 1, SSM_LANES), lbi.reshape(2, 1, SSM_LANES)], axis=-1)
    return bcat, lam, ccat


def _head_perm():
    cols = []
    for p in range(Q_PER_KV):
        for h in (p, Q_PER_KV + p):
            cols.extend(range(h * HEAD_DIM, (h + 1) * HEAD_DIM))
    return jnp.array(cols, dtype=jnp.int32)


def _rope_tables(n_ctx, n_lat):
    t = jnp.arange(n_lat)
    lane = jnp.arange(LANE)
    hl = lane % HEAD_DIM
    quarter = HEAD_DIM // 4
    inv_freq = ROPE_BASE ** (-jnp.arange(quarter, dtype=F32) / quarter)
    pos = jnp.where(hl[None, :] < HEAD_DIM // 2, (t // GRID_W)[:, None], (t % GRID_W)[:, None])
    ang = pos.astype(F32) * inv_freq[hl % quarter][None, :]
    first_half = (hl % (2 * quarter)) < quarter
    cos_l = jnp.cos(ang)
    sin_l = jnp.where(first_half[None, :], -jnp.sin(ang), jnp.sin(ang))
    cos_t = jnp.concatenate([jnp.ones((n_ctx, LANE), F32), cos_l], axis=0)
    sin_t = jnp.concatenate([jnp.zeros((n_ctx, LANE), F32), sin_l], axis=0)
    return cos_t, sin_t


def kernel(x, c, ctx, c_ctx, w_ada, b_ada, g_pre_mix, g_post_mix, g_pre_ffn, g_post_ffn, w_in, pool_w, pool_scale, lam_re, lam_im, log_dt, b_re, b_im, c_re, c_im, s5_d, w_glu, b_glu, sink, w_out, w_up, conv_w, conv_b, w_down):
    B, L, _ = x.shape
    N = ctx.shape[1]
    S = N + L
    assert N % TOK_TILE == 0 and L % TOK_TILE == 0 and L % GRID_W == 0

    rows = -(-(B + 1) // 8) * 8
    cond = jnp.concatenate([c, c_ctx[None, :], jnp.zeros((rows - B - 1, D_MODEL), F32)], axis=0)
    mod_all = _ada(cond, w_ada, b_ada)
    mod_lat = mod_all[:, :B].reshape(DEPTH, B, 1, 6, D_MODEL)
    mod_ctx = jnp.broadcast_to(mod_all[:, B].reshape(DEPTH, 1, 1, 6, D_MODEL), (DEPTH, B, 1, 6, D_MODEL))
    mods = jnp.concatenate([mod_ctx, mod_lat], axis=2)

    cos_t, sin_t = _rope_tables(N, L)
    perm = _head_perm()
    q0 = POOL_WIDTH + SSM_WIDTH
    in_cols = jnp.concatenate([jnp.arange(q0), q0 + perm, jnp.arange(q0 + ATTN_WIDTH, D_IN)])
    out_rows = jnp.concatenate([jnp.arange(q0), q0 + perm])
    eye_w = jnp.eye(len(POOL_WINDOWS), dtype=F32)
    nch = D_FF // FF_CHUNK

    def ffn_chunks(t):
        r = t.shape[0]
        return t.reshape(r, 2, nch, FF_CHUNK).transpose(2, 0, 1, 3).reshape(nch, r, 2 * FF_CHUNK)

    stream = (ctx, x)
    for l in range(DEPTH):
        last = l == DEPTH - 1
        w_in_l = w_in[l][:, in_cols].astype(BF16)
        w_out_l = w_out[l][out_rows].astype(BF16)
        pool_blk = jnp.einsum('gcd,gk->gckd', pool_w[l], eye_w).reshape(POOL_WIDTH, POOL_WIDTH).astype(BF16)
        bcat, lam, ccat = _zoh_blocks(lam_re[l], lam_im[l], log_dt[l], b_re[l], b_im[l], c_re[l], c_im[l])
        sink_b = jnp.broadcast_to(sink[l][:, None] * LOG2_E, (N_Q_HEADS, LANE))
        wu = ffn_chunks(w_up[l]).astype(BF16)
        cw = ffn_chunks(conv_w[l])
        cb = ffn_chunks(conv_b[l][None, :])
        wd = w_down[l].astype(BF16)

        pool_in, us, q, kt, v = _in_proj(*stream, S, mods[l], g_pre_mix[l][None, :], w_in_l, cos_t, sin_t,
                                         N // TOK_TILE)
        a = _pool(pool_in, pool_blk, pool_scale[l][None, :], N)
        bs = _s5(us.reshape(S * B, SSM_WIDTH), bcat, lam, ccat, s5_d[l][None, :],
                 w_glu[l].astype(BF16), b_glu[l][None, :], B, N)
        att = _attention(q, kt, v, sink_b, N, last)
        out = _layer_tail(a, bs.reshape(S, B * SSM_WIDTH), att, *stream, mods[l],
                          w_out_l[:POOL_WIDTH], w_out_l[POOL_WIDTH:q0], w_out_l[q0:],
                          g_post_mix[l][None, :], g_pre_ffn[l][None, :],
                          wu, cw, cb, wd, g_post_ffn[l][None, :], N, last)
        stream = (out, None)
    return out
```

```python
import functools
import math

import jax
import jax.numpy as jnp
from jax import lax
from jax.experimental import pallas as pl
from jax.experimental.pallas import tpu as pltpu

F32 = jnp.float32
BF16 = jnp.bfloat16

D_MODEL = 1024
DEPTH = 4
GRID_W = 64
EPS = 1e-6
POOL_WINDOWS = (2, 4, 8, 16)
POOL_WIDTH = 256
POOL_GROUP = 64
SSM_WIDTH = 256
SSM_CH = 16
SSM_GROUPS = 16
SSM_STATE = 64
SSM_LANES = SSM_GROUPS * SSM_STATE
HEAD_DIM = 64
ATTN_WIDTH = 512
N_Q_HEADS = 8
N_KV_HEADS = 2
Q_PER_KV = 4
KV_WIDTH = 128
WINDOW = 128
BLOCK = 128
ROPE_BASE = 10000.0
D_FF = 2816
D_IN = 1280
NEG_INF = -1e30
LOG2_E = 1.4426950408889634

LANE = 128
POOL_PAD = 16
POOL_HALO = 8
POOL_CHUNK = 128
TOK_TILE = 256
S5_STEPS = 16
FF_CHUNK = 256
HALO_ROWS = 16
SUBLANES = 8
IN_SPLIT = 2
ATTN_UNIT_PAIRS = ((0, 1, 2, 3),)
VMEM_LIMIT = 56 * 1024 * 1024


def _cparams(sem):
    return pltpu.CompilerParams(dimension_semantics=sem, vmem_limit_bytes=VMEM_LIMIT)


def _rms(x, g):
    return x * lax.rsqrt(jnp.mean(x * x, axis=-1, keepdims=True) + EPS) * g


def _sigmoid(x):
    return 1.0 / (1.0 + jnp.exp(-x))


def _gelu_tanh(x):
    c = math.sqrt(2.0 / math.pi)
    return 0.5 * x * (1.0 + jnp.tanh(c * (x + 0.044715 * (x * x * x))))


def _ada_kernel(c_ref, w_ref, b_ref, o_ref):
    c = c_ref[...]
    s = c * _sigmoid(c)
    o_ref[...] = jnp.dot(s.astype(BF16), w_ref[...].astype(BF16), preferred_element_type=F32) + b_ref[...]


def _ada(cond, w_ada, b_ada):
    rows = cond.shape[0]
    nblk = w_ada.shape[-1] // D_MODEL
    return pl.pallas_call(
        _ada_kernel,
        out_shape=jax.ShapeDtypeStruct((DEPTH, rows, nblk * D_MODEL), F32),
        grid=(DEPTH, nblk),
        in_specs=[
            pl.BlockSpec((rows, D_MODEL), lambda l, j: (0, 0)),
            pl.BlockSpec((None, D_MODEL, D_MODEL), lambda l, j: (l, 0, j)),
            pl.BlockSpec((None, 1, D_MODEL), lambda l, j: (l, 0, j)),
        ],
        out_specs=pl.BlockSpec((None, rows, D_MODEL), lambda l, j: (l, 0, j)),
        compiler_params=_cparams(("arbitrary", "arbitrary")),
        name="ada_mod",
    )(cond, w_ada, b_ada.reshape(DEPTH, 1, -1))


def _in_proj_kernel(*refs, nct, dual):
    x_refs, rest = refs[:2 if dual else 1], refs[2 if dual else 1:]
    mod_ref, g_ref, w_ref, cos_ref, sin_ref, pool_ref, us_ref, q_ref, kt_ref, v_ref = rest
    is_ctx = pl.program_id(1) < nct
    T = x_refs[0].shape[0]
    lane = lax.broadcasted_iota(jnp.int32, (T // IN_SPLIT, LANE), 1)
    first_half = (lane % 32) < 16
    q0 = POOL_WIDTH + SSM_WIDTH
    k0 = q0 + ATTN_WIDTH
    scale = HEAD_DIM ** -0.5 * LOG2_E
    w = w_ref[...]
    for part in range(IN_SPLIT):
        rows = slice(part * (T // IN_SPLIT), (part + 1) * (T // IN_SPLIT))
        x = x_refs[0][rows, :]
        if dual:
            x = jnp.where(is_ctx, x, x_refs[1][rows, :])
        h = _rms(x, g_ref[...] * (1.0 + mod_ref[1:2, :])) + mod_ref[0:1, :]
        z = jnp.dot(h.astype(BF16), w, preferred_element_type=F32)
        pool_ref[rows, :] = z[:, 0:POOL_WIDTH]
        us_ref[rows, :] = z[:, POOL_WIDTH:POOL_WIDTH + SSM_WIDTH]
        cos = cos_ref[rows, :]
        sin = sin_ref[rows, :]

        def rope(t):
            partner = jnp.where(first_half, pltpu.roll(t, LANE - 16, axis=1), pltpu.roll(t, 16, axis=1))
            return t * cos + partner * sin

        for j in range(ATTN_WIDTH // LANE):
            t = z[:, q0 + j * LANE:q0 + (j + 1) * LANE]
            q_ref[rows, j * LANE:(j + 1) * LANE] = (rope(t) * scale).astype(BF16)
        kt_ref[:, rows] = rope(z[:, k0:k0 + KV_WIDTH]).T.astype(BF16)
        v_ref[rows, :] = z[:, k0 + KV_WIDTH:k0 + 2 * KV_WIDTH].astype(BF16)


def _stream_views(ctx_src, lat_src, nct):
    if lat_src is None:
        return (ctx_src,), (lambda i: i,)
    return (ctx_src, lat_src), (lambda i: jnp.minimum(i, nct - 1), lambda i: jnp.maximum(i - nct, 0))


def _in_proj(ctx_src, lat_src, S, mod, g, w_in, cos_t, sin_t, nct):
    B = ctx_src.shape[0]
    T = TOK_TILE
    grid = (B, S // T)
    tok = lambda width: pl.BlockSpec((None, T, width), lambda b, i: (b, i, 0))
    srcs, views = _stream_views(ctx_src, lat_src, nct)
    x_specs = [pl.BlockSpec((None, T, D_MODEL), lambda b, i, view=view: (b, view(i), 0)) for view in views]
    return pl.pallas_call(
        functools.partial(_in_proj_kernel, nct=nct, dual=len(srcs) == 2),
        out_shape=(
            jax.ShapeDtypeStruct((B, S, POOL_WIDTH), F32),
            jax.ShapeDtypeStruct((S, B * SSM_WIDTH), F32),
            jax.ShapeDtypeStruct((B, S, ATTN_WIDTH), BF16),
            jax.ShapeDtypeStruct((B, KV_WIDTH, S), BF16),
            jax.ShapeDtypeStruct((B, S, KV_WIDTH), BF16),
        ),
        grid=grid,
        in_specs=x_specs + [
            pl.BlockSpec((None, None, 6, D_MODEL), lambda b, i: (b, jnp.where(i >= nct, 1, 0), 0, 0)),
            pl.BlockSpec((1, D_MODEL), lambda b, i: (0, 0)),
            pl.BlockSpec((D_MODEL, D_IN), lambda b, i: (0, 0)),
            pl.BlockSpec((T, LANE), lambda b, i: (i, 0)),
            pl.BlockSpec((T, LANE), lambda b, i: (i, 0)),
        ],
        out_specs=(
            tok(POOL_WIDTH),
            pl.BlockSpec((T, SSM_WIDTH), lambda b, i: (i, b)),
            tok(ATTN_WIDTH),
            pl.BlockSpec((None, KV_WIDTH, T), lambda b, i: (b, 0, i)),
            tok(KV_WIDTH),
        ),
        compiler_params=_cparams(("parallel", "parallel")),
        name="in_proj",
    )(*srcs, mod, g, w_in, cos_t, sin_t)


def _pool_kernel(u_ref, w_ref, sc_ref, o_ref, pad_ref, *, n_ctx, n_lat):
    C = POOL_CHUNK
    zeros = jnp.zeros((POOL_PAD, POOL_WIDTH), F32)
    ctx0 = POOL_PAD
    lat0 = 2 * POOL_PAD + n_ctx
    pad_ref[0:POOL_PAD, :] = zeros
    pad_ref[ctx0 + n_ctx:lat0, :] = zeros
    pad_ref[lat0 + n_lat:lat0 + n_lat + POOL_PAD, :] = zeros
    pad_ref[ctx0:ctx0 + n_ctx, :] = u_ref[0:n_ctx, :]
    pad_ref[lat0:lat0 + n_lat, :] = u_ref[n_ctx:n_ctx + n_lat, :]

    nw = len(POOL_WINDOWS)
    t_idx = lax.broadcasted_iota(jnp.int32, (C, C + 2 * POOL_HALO), 0)
    s_idx = lax.broadcasted_iota(jnp.int32, (C, C + 2 * POOL_HALO), 1) - POOL_HALO
    band = jnp.concatenate(
        [jnp.where((s_idx >= t_idx - w // 2) & (s_idx < t_idx - w // 2 + w), 1.0, 0.0).astype(BF16)
         for w in POOL_WINDOWS], axis=0)
    grp = lax.broadcasted_iota(jnp.int32, (C, POOL_WIDTH), 1) // POOL_GROUP
    row = lax.broadcasted_iota(jnp.int32, (C, POOL_WIDTH), 0)
    w_lane = functools.reduce(lambda acc, gw: jnp.where(grp == gw[0], gw[1], acc),
                              list(enumerate(POOL_WINDOWS))[1:], jnp.full((C, POOL_WIDTH), POOL_WINDOWS[0]))
    inv_w = 1.0 / w_lane.astype(F32)
    wmat = w_ref[...]
    scale = sc_ref[...]

    def window_sums(c, pad_base):
        base = pad_base + c * C
        win = pad_ref[base - POOL_HALO:base + C + POOL_HALO, :]
        hi = win.astype(BF16)
        lo = (win - hi.astype(F32)).astype(BF16)
        return (jnp.dot(band, hi, preferred_element_type=F32)
                + jnp.dot(band, lo, preferred_element_type=F32))

    def finish(sums, c, pad_base, out_base, seq_len):
        base = pad_base + c * C
        pooled = sums[0:C, :]
        for gi in range(1, nw):
            pooled = jnp.where(grp == gi, sums[gi * C:(gi + 1) * C, :], pooled)
        if c * C >= POOL_HALO and (c + 1) * C + POOL_HALO <= seq_len:
            pooled = pooled * inv_w
        else:
            start = row + c * C - w_lane // 2
            cnt = jnp.minimum(start + w_lane, seq_len) - jnp.maximum(start, 0)
            pooled = pooled / cnt.astype(F32)
        pooled = pooled - pad_ref[base:base + C, :]
        mixed = jnp.dot(pooled.astype(BF16), wmat, preferred_element_type=F32) * scale
        o_ref[out_base + c * C:out_base + (c + 1) * C, :] = mixed.astype(BF16)

    jobs = ([(c, ctx0, 0, n_ctx) for c in range(n_ctx // C)]
            + [(c, lat0, n_ctx, n_lat) for c in range(n_lat // C)])
    sums = window_sums(jobs[0][0], jobs[0][1])
    for k, job in enumerate(jobs):
        nxt = window_sums(jobs[k + 1][0], jobs[k + 1][1]) if k + 1 < len(jobs) else None
        finish(sums, *job)
        sums = nxt


def _pool(pool_in, w_blk, scale, n_ctx):
    B, S, _ = pool_in.shape
    n_lat = S - n_ctx
    return pl.pallas_call(
        functools.partial(_pool_kernel, n_ctx=n_ctx, n_lat=n_lat),
        out_shape=jax.ShapeDtypeStruct((B, S, POOL_WIDTH), BF16),
        grid=(B,),
        in_specs=[
            pl.BlockSpec((None, S, POOL_WIDTH), lambda b: (b, 0, 0)),
            pl.BlockSpec((POOL_WIDTH, POOL_WIDTH), lambda b: (0, 0)),
            pl.BlockSpec((1, POOL_WIDTH), lambda b: (0, 0)),
        ],
        out_specs=pl.BlockSpec((None, S, POOL_WIDTH), lambda b: (b, 0, 0)),
        scratch_shapes=[pltpu.VMEM((S + 3 * POOL_PAD, POOL_WIDTH), F32)],
        compiler_params=_cparams(("parallel",)),
        name="pool_mixer",
    )(pool_in, w_blk, scale)


def _s5_scan(lam_ref, bu_ref, hb_ref, state, *, batch, reverse):
    for cb in range(SSM_LANES // LANE):
        re = slice(cb * LANE, (cb + 1) * LANE)
        im = slice(SSM_LANES + cb * LANE, SSM_LANES + (cb + 1) * LANE)
        lr = jnp.broadcast_to(lam_ref[0:1, re], (batch, LANE))
        li = jnp.broadcast_to(lam_ref[0:1, im], (batch, LANE))
        s_re = state[:, re]
        s_im = state[:, im]
        for step in range(S5_STEPS):
            t = S5_STEPS - 1 - step if reverse else step
            rows = slice(t * batch, (t + 1) * batch)
            n_re = lr * s_re - li * s_im + bu_ref[rows, re]
            n_im = lr * s_im + li * s_re + bu_ref[rows, im]
            hb_ref[rows, re] = n_re.astype(BF16)
            hb_ref[rows, im] = n_im.astype(BF16)
            s_re, s_im = n_re, n_im
        state[:, re] = s_re
        state[:, im] = s_im


def _s5_pair(u_ref, bcat_ref, lam_ref, ccat_ref, bu0, bu1, hb0, hb1, state, *, batch, reverse):
    R = S5_STEPS * batch

    @pl.when(pl.program_id(0) == 0)
    def _():
        state[...] = jnp.zeros(state.shape, F32)
        for ref in (bu0, bu1, hb0, hb1):
            ref[...] = jnp.zeros(ref.shape, ref.dtype)

    first, second = (slice(R, 2 * R), slice(0, R)) if reverse else (slice(0, R), slice(R, 2 * R))
    bcat = bcat_ref[...]
    ccat = ccat_ref[...]
    y_a = jnp.dot(hb0[...], ccat, preferred_element_type=F32)
    bu0[...] = jnp.dot(u_ref[first, :].astype(BF16), bcat, preferred_element_type=F32)
    _s5_scan(lam_ref, bu1, hb1, state, batch=batch, reverse=reverse)
    bu1[...] = jnp.dot(u_ref[second, :].astype(BF16), bcat, preferred_element_type=F32)
    y_b = jnp.dot(hb1[...], ccat, preferred_element_type=F32)
    _s5_scan(lam_ref, bu0, hb0, state, batch=batch, reverse=reverse)
    return jnp.concatenate([y_b, y_a] if reverse else [y_a, y_b], axis=0)


def _s5_fwd_kernel(u_ref, bcat_ref, lam_ref, ccat_ref, y_ref, bu0, bu1, hb0, hb1, state, *, batch):
    y_ref[...] = _s5_pair(u_ref, bcat_ref, lam_ref, ccat_ref, bu0, bu1, hb0, hb1, state,
                          batch=batch, reverse=False)


def _s5_bwd_kernel(u_ref, bcat_ref, lam_ref, ccat_ref, u2_ref, yf_ref, dsk_ref, wglu_ref, bglu_ref,
                   o_ref, bu0, bu1, hb0, hb1, state, *, batch):
    y = _s5_pair(u_ref, bcat_ref, lam_ref, ccat_ref, bu0, bu1, hb0, hb1, state, batch=batch, reverse=True)
    tot = yf_ref[...] + y + dsk_ref[...] * u2_ref[...]
    g = _gelu_tanh(tot)
    gate = jnp.dot(g.astype(BF16), wglu_ref[...], preferred_element_type=F32) + bglu_ref[...]
    o_ref[...] = (g * _sigmoid(gate)).astype(BF16)


def _s5(us_tm, bcat, lam, ccat, dskip, w_glu, b_glu, batch, n_ctx):
    rows_total = us_tm.shape[0]
    R = S5_STEPS * batch
    nch = rows_total // (2 * R)
    nctx = n_ctx // (2 * S5_STEPS)
    assert rows_total % (2 * R) == 0 and n_ctx % (2 * S5_STEPS) == 0
    const = lambda shape: pl.BlockSpec(shape, lambda j: (0,) * len(shape))

    def dir_spec(d, shape):
        return pl.BlockSpec((None,) + shape, lambda j: (d,) + (0,) * len(shape))

    def chunk_spec(order, lag):
        return pl.BlockSpec((2 * R, SSM_WIDTH), lambda j: (order(jnp.clip(j - lag, 0, nch - 1)), 0))

    fwd = lambda c: c
    rev = lambda c: jnp.where(c < nctx, nctx - 1 - c, nch + nctx - 1 - c)

    scratch = [pltpu.VMEM((R, 2 * SSM_LANES), F32), pltpu.VMEM((R, 2 * SSM_LANES), F32),
               pltpu.VMEM((R, 2 * SSM_LANES), BF16), pltpu.VMEM((R, 2 * SSM_LANES), BF16),
               pltpu.VMEM((batch, 2 * SSM_LANES), F32)]
    yf = pl.pallas_call(
        functools.partial(_s5_fwd_kernel, batch=batch),
        out_shape=jax.ShapeDtypeStruct((rows_total, SSM_WIDTH), F32),
        grid=(nch + 1,),
        in_specs=[
            chunk_spec(fwd, 0),
            dir_spec(0, (SSM_WIDTH, 2 * SSM_LANES)),
            dir_spec(0, (1, 2 * SSM_LANES)),
            dir_spec(0, (2 * SSM_LANES, SSM_WIDTH)),
        ],
        out_specs=chunk_spec(fwd, 1),
        scratch_shapes=scratch,
        compiler_params=_cparams(("arbitrary",)),
        name="s5_forward",
    )(us_tm, bcat, lam, ccat)

    return pl.pallas_call(
        functools.partial(_s5_bwd_kernel, batch=batch),
        out_shape=jax.ShapeDtypeStruct((rows_total, SSM_WIDTH), BF16),
        grid=(nch + 1,),
        in_specs=[
            chunk_spec(rev, 0),
            dir_spec(1, (SSM_WIDTH, 2 * SSM_LANES)),
            dir_spec(1, (1, 2 * SSM_LANES)),
            dir_spec(1, (2 * SSM_LANES, SSM_WIDTH)),
            chunk_spec(rev, 1),
            chunk_spec(rev, 1),
            const((1, SSM_WIDTH)),
            const((SSM_WIDTH, SSM_WIDTH)),
            const((1, SSM_WIDTH)),
        ],
        out_specs=chunk_spec(rev, 1),
        scratch_shapes=scratch,
        compiler_params=_cparams(("arbitrary",)),
        name="s5_backward",
    )(us_tm, bcat, lam, ccat, us_tm, yf, dskip, w_glu, b_glu)


def _attn_kernel(q_ref, kc_ref, vc_ref, kp_ref, kq_ref, kn_ref, vp_ref, vq_ref, vn_ref, sink_ref,
                 o_ref, s_ref, e_ref, *, nct, n_lat, tile0):
    i = pl.program_id(1) + tile0
    low = lax.broadcasted_iota(jnp.int32, (BLOCK, LANE), 1) < HEAD_DIM

    def run(kcat, vcat, biases):
        low_v = lax.broadcasted_iota(jnp.int32, vcat.shape, 1) < HEAD_DIM
        one = jnp.ones_like(vcat)
        vones = (jnp.where(low_v, vcat, one), jnp.where(low_v, one, vcat))
        npair = ATTN_WIDTH // LANE
        nk = kcat.shape[1]
        qs = [q_ref[:, p * LANE:(p + 1) * LANE] for p in range(npair)]
        zero = jnp.zeros_like(qs[0])
        units = [(half, pairs) for pairs in ATTN_UNIT_PAIRS for half in (0, 1)]
        per_unit = len(ATTN_UNIT_PAIRS[0])
        stacked = [None if b is None else jnp.concatenate([b] * per_unit, axis=0) for b in biases]

        def scores(u):
            half, pairs = units[u]
            q = jnp.concatenate([jnp.where(low, qs[p], zero) if half == 0 else jnp.where(low, zero, qs[p])
                                 for p in pairs], axis=0)
            s_ref[u, :, :nk] = jnp.dot(q, kcat, preferred_element_type=F32)

        def softmax(u):
            half, pairs = units[u]
            sink = jnp.concatenate(
                [jnp.broadcast_to(sink_ref[half * Q_PER_KV + p:half * Q_PER_KV + p + 1, 0:1], (BLOCK, 1))
                 for p in pairs], axis=0)
            tiles = []
            for t, bias in enumerate(stacked):
                st = s_ref[u, :, t * LANE:(t + 1) * LANE]
                tiles.append(st if bias is None else st + bias)
            mx = functools.reduce(jnp.maximum, tiles)
            m = jnp.maximum(jnp.max(mx, axis=-1, keepdims=True), sink)
            for t, st in enumerate(tiles):
                e_ref[u, :, t * LANE:(t + 1) * LANE] = jnp.exp2(st - m).astype(BF16)
            return jnp.exp2(sink - m)

        def values(u):
            return jnp.dot(e_ref[u, :, :nk], vones[units[u][0]], preferred_element_type=F32)

        nu = len(units)
        sink_terms, accs = [None] * nu, [None] * nu
        scores(0)
        scores(1)
        for u in range(nu):
            sink_terms[u] = softmax(u)
            if u + 2 < nu:
                scores(u + 2)
            accs[u] = values(u)
        low2 = jnp.concatenate([low] * per_unit, axis=0)
        for g in range(nu // 2):
            a_lo, a_hi = accs[2 * g], accs[2 * g + 1]
            num = jnp.where(low2, a_lo, a_hi)
            den = (jnp.where(low2, pltpu.roll(a_lo, HEAD_DIM, axis=1), pltpu.roll(a_hi, HEAD_DIM, axis=1))
                   + jnp.where(low2, sink_terms[2 * g], sink_terms[2 * g + 1]))
            out = (num / den).astype(BF16)
            for r, p in enumerate(units[2 * g][1]):
                o_ref[:, p * LANE:(p + 1) * LANE] = out[r * BLOCK:(r + 1) * BLOCK, :]

    @pl.when(i < nct)
    def _():
        run(kc_ref[...], vc_ref[...], [None] * (kc_ref.shape[1] // LANE))

    @pl.when(i >= nct)
    def _():
        li = i - nct
        r = lax.broadcasted_iota(jnp.int32, (BLOCK, BLOCK), 0)
        c = lax.broadcasted_iota(jnp.int32, (BLOCK, BLOCK), 1)
        bias_p = jnp.where((c >= r) & (li >= 1), 0.0, NEG_INF)
        bias_n = jnp.where((c <= r) & ((li + 2) * BLOCK <= n_lat), 0.0, NEG_INF)
        kcat = jnp.concatenate([kc_ref[...], kp_ref[...], kq_ref[...], kn_ref[...]], axis=1)
        vcat = jnp.concatenate([vc_ref[...], vp_ref[...], vq_ref[...], vn_ref[...]], axis=0)
        run(kcat, vcat, [None] * (kc_ref.shape[1] // LANE) + [bias_p, None, bias_n])


def _attention(q, kt, v, sink_b, n_ctx, lat_only):
    B, S, _ = q.shape
    nct = n_ctx // BLOCK
    ntiles = S // BLOCK
    tile0 = nct if lat_only else 0
    rows_out = S - tile0 * BLOCK
    nunit = 2 * len(ATTN_UNIT_PAIRS)
    unit_rows = len(ATTN_UNIT_PAIRS[0]) * BLOCK
    cur = lambda i: i + tile0
    prev = lambda i: jnp.maximum(i + tile0 - 1, nct)
    nxt = lambda i: jnp.minimum(i + tile0 + 1, ntiles - 1)
    vblk = lambda f: pl.BlockSpec((None, BLOCK, KV_WIDTH), lambda b, i: (b, f(i), 0))
    kblk = lambda f: pl.BlockSpec((None, KV_WIDTH, BLOCK), lambda b, i: (b, 0, f(i)))
    return pl.pallas_call(
        functools.partial(_attn_kernel, nct=nct, n_lat=S - n_ctx, tile0=tile0),
        out_shape=jax.ShapeDtypeStruct((B, rows_out, ATTN_WIDTH), BF16),
        grid=(B, ntiles - tile0),
        in_specs=[
            pl.BlockSpec((None, BLOCK, ATTN_WIDTH), lambda b, i: (b, cur(i), 0)),
            pl.BlockSpec((None, KV_WIDTH, n_ctx), lambda b, i: (b, 0, 0)),
            pl.BlockSpec((None, n_ctx, KV_WIDTH), lambda b, i: (b, 0, 0)),
            kblk(prev), kblk(cur), kblk(nxt),
            vblk(prev), vblk(cur), vblk(nxt),
            pl.BlockSpec((N_Q_HEADS, LANE), lambda b, i: (0, 0)),
        ],
        out_specs=pl.BlockSpec((None, BLOCK, ATTN_WIDTH), lambda b, i: (b, i, 0)),
        scratch_shapes=[pltpu.VMEM((nunit, unit_rows, n_ctx + 3 * BLOCK), F32),
                        pltpu.VMEM((nunit, unit_rows, n_ctx + 3 * BLOCK), BF16)],
        compiler_params=_cparams(("parallel", "parallel")),
        name="attention",
    )(q, kt, v, kt, kt, kt, v, v, v, sink_b)


def _interleave_rows(h2, tmp_ref, dst_ref):
    nl = tmp_ref.shape[0]
    T = h2.shape[0]
    G = T // SUBLANES
    pitch = G + SUBLANES
    for c in range(nl):
        for s in range(SUBLANES):
            tmp_ref[c, s * pitch:s * pitch + G, :] = h2[s * G:(s + 1) * G, c * LANE:(c + 1) * LANE]
    pack = 2
    for v in range(0, G, pack):
        blk = jnp.concatenate(
            [jnp.concatenate([tmp_ref[c, pl.ds(v + d, SUBLANES, stride=pitch), :] for c in range(nl)], axis=1)
             for d in range(pack)], axis=0)
        dst_ref[v * SUBLANES:(v + pack) * SUBLANES, :] = blk.astype(BF16)


def _tail_kernel(a_ref, ap_ref, an_ref, b_ref, bp_ref, bn_ref, c_ref, cp_ref, cn_ref, *refs,
                 ntiles, ctx_tiles, dual, seq_starts, seq_ends):
    n = pl.program_id(0)
    nx = 6 if dual else 3
    x_refs, rest = refs[:nx], refs[nx:]
    (mod_ref, modp_ref, wa_ref, wb_ref, wc_ref, gpost_ref, gpre_ref, wu_ref, cw_ref, cb_ref, wd_ref, g_ref,
     o_ref, act_ref, y_ref, tmp_ref, lhs_nxt, x1_nxt, lhs_ref, x1_ref, yp_ref) = rest
    nsteps = pl.num_programs(0)
    T = x_refs[0].shape[0]
    G = T // SUBLANES
    half = HALO_ROWS // 2

    @pl.when(n == 0)
    def _():
        lhs_ref[...] = jnp.zeros(lhs_ref.shape, BF16)
        x1_ref[...] = jnp.zeros(x1_ref.shape, F32)

    i = lax.rem(jnp.maximum(n - 1, 0), ntiles)
    first = functools.reduce(jnp.logical_or, [i == s for s in seq_starts])
    last = functools.reduce(jnp.logical_or, [i == e for e in seq_ends])
    lhs = lhs_ref[...]
    sub = lax.broadcasted_iota(jnp.int32, (SUBLANES, 2 * FF_CHUNK), 0)

    for j in range(wu_ref.shape[0]):
        ue = jnp.dot(lhs, wu_ref[j], preferred_element_type=F32)
        u = ue[:T, :]
        edge = ue[T:, :]
        prev_row = jnp.where(first, 0.0, edge[half - 1:half, :])
        next_row = jnp.where(last, 0.0, edge[half:half + 1, :])
        head = jnp.where(sub == 0, prev_row, pltpu.roll(u[T - SUBLANES:, :], 1, axis=0))
        tail = jnp.where(sub == SUBLANES - 1, next_row, pltpu.roll(u[:SUBLANES, :], SUBLANES - 1, axis=0))
        up = jnp.concatenate([head, u[:T - SUBLANES, :]], axis=0)
        un = jnp.concatenate([u[SUBLANES:, :], tail], axis=0)
        cw = cw_ref[j]
        c = cw[0:1, :] * up + cw[1:2, :] * u + cw[2:3, :] * un + cb_ref[j]
        val = c[:, :FF_CHUNK]
        gate = c[:, FF_CHUNK:]
        act = gate * _sigmoid(gate) * val
        act_ref[:, j * FF_CHUNK:(j + 1) * FF_CHUNK] = act.astype(BF16)

    def ext(ref, prev_ref, next_ref):
        return jnp.concatenate([ref[...], prev_ref[half:, :], next_ref[:half, :]], axis=0)

    yp_ref[...] = (jnp.dot(ext(a_ref, ap_ref, an_ref), wa_ref[...], preferred_element_type=F32)
                   + jnp.dot(ext(b_ref, bp_ref, bn_ref), wb_ref[...], preferred_element_type=F32)
                   + jnp.dot(ext(c_ref, cp_ref, cn_ref), wc_ref[...], preferred_element_type=F32))
    y = jnp.dot(act_ref[...], wd_ref[...], preferred_element_type=F32)

    x_ext = jnp.concatenate([r[...] for r in x_refs[:3]], axis=0)
    if dual:
        is_ctx = lax.rem(jnp.minimum(n, nsteps - 2), ntiles) < ctx_tiles
        x_ext = jnp.where(is_ctx, x_ext, jnp.concatenate([r[...] for r in x_refs[3:]], axis=0))
    x1 = x_ext + _rms(yp_ref[...], gpost_ref[...] * mod_ref[2:3, :])
    x1_nxt[...] = x1[:T, :]
    h2 = _rms(x1, gpre_ref[...] * (1.0 + mod_ref[4:5, :])) + mod_ref[3:4, :]
    _interleave_rows(h2[:T, :], tmp_ref, lhs_nxt)
    lhs_nxt[T:, :] = h2[T:, :].astype(BF16)

    r = _rms(y, g_ref[...] * modp_ref[5:6, :])
    nl = y_ref.shape[0]
    for c in range(nl):
        y_ref[c] = r[:, c * LANE:(c + 1) * LANE]
    for k in range(G):
        src = pl.ds((SUBLANES * k % G) * SUBLANES + SUBLANES * k // G, SUBLANES, stride=SUBLANES)
        rows = slice(k * SUBLANES, (k + 1) * SUBLANES)
        yk = jnp.concatenate([y_ref[c, src, :] for c in range(nl)], axis=1)
        o_ref[rows, :] = x1_ref[rows, :] + yk
    lhs_ref[...] = lhs_nxt[...]
    x1_ref[...] = x1_nxt[...]


def _layer_tail(a, bs, att, ctx_src, lat_src, mod, wa, wb, wc, gpost, gpre, wu, cw, cb, wd, g, n_ctx, lat_only):
    B, S, _ = a.shape
    T = TOK_TILE
    nct = n_ctx // T
    tile0 = nct if lat_only else 0
    ntiles = S // T - tile0
    srcs, views = _stream_views(ctx_src, lat_src, nct)
    seq_starts = (0,) if lat_only else (0, nct)
    seq_ends = (ntiles - 1,) if lat_only else (nct - 1, ntiles - 1)
    const = lambda shape: pl.BlockSpec(shape, lambda n: (0,) * len(shape), pipeline_mode=pl.Buffered(1))
    ntot = B * ntiles

    def tile(n, lag=0):
        t = jnp.clip(n - lag, 0, ntot - 1)
        return t // ntiles, lax.rem(t, ntiles)

    def tok(width, rows_total, off, blk=T, view=lambda i: i):
        per = T // blk
        nblk = rows_total // blk

        def at(f):
            def index_map(n):
                b, i = tile(n)
                return b, f(view(i + off)), 0
            return index_map

        return [
            pl.BlockSpec((None, T, width), at(lambda i: i)),
            pl.BlockSpec((None, blk, width), at(lambda i: jnp.maximum(i * per - 1, 0))),
            pl.BlockSpec((None, blk, width), at(lambda i: jnp.minimum((i + 1) * per, nblk - 1))),
        ]

    per = T // HALO_ROWS
    nblk = S // HALO_ROWS

    def bs_at(f):
        def index_map(n):
            b, i = tile(n)
            return f(i + tile0), b
        return index_map

    bs_specs = [
        pl.BlockSpec((T, SSM_WIDTH), bs_at(lambda i: i)),
        pl.BlockSpec((HALO_ROWS, SSM_WIDTH), bs_at(lambda i: jnp.maximum(i * per - 1, 0))),
        pl.BlockSpec((HALO_ROWS, SSM_WIDTH), bs_at(lambda i: jnp.minimum((i + 1) * per, nblk - 1))),
    ]

    def mod_spec(lag):
        def index_map(n):
            b, i = tile(n, lag)
            return b, jnp.where(i + tile0 >= nct, 1, 0), 0, 0
        return pl.BlockSpec((None, None, 6, D_MODEL), index_map)

    def out_map(n):
        b, i = tile(n, 1)
        return b, i, 0

    att_off = 0 if lat_only else tile0
    nch = wu.shape[0]
    return pl.pallas_call(
        functools.partial(_tail_kernel, ntiles=ntiles, ctx_tiles=nct - tile0, dual=len(srcs) == 2,
                          seq_starts=seq_starts, seq_ends=seq_ends),
        out_shape=jax.ShapeDtypeStruct((B, ntiles * T, D_MODEL), F32),
        grid=(ntot + 1,),
        in_specs=(
            tok(POOL_WIDTH, S, tile0, HALO_ROWS) + bs_specs + tok(ATTN_WIDTH, att.shape[1], att_off, HALO_ROWS)
            + [spec for src, view in zip(srcs, views)
               for spec in tok(D_MODEL, src.shape[1], tile0, SUBLANES, view)]
            + [mod_spec(0), mod_spec(1),
               const((POOL_WIDTH, D_MODEL)), const((SSM_WIDTH, D_MODEL)), const((ATTN_WIDTH, D_MODEL)),
               const((1, D_MODEL)), const((1, D_MODEL)),
               const((nch, D_MODEL, 2 * FF_CHUNK)),
               const((nch, 3, 2 * FF_CHUNK)),
               const((nch, 1, 2 * FF_CHUNK)),
               const((D_FF, D_MODEL)),
               const((1, D_MODEL))]),
        out_specs=pl.BlockSpec((None, T, D_MODEL), out_map),
        scratch_shapes=[pltpu.VMEM((T, D_FF), BF16),
                        pltpu.VMEM((D_MODEL // LANE, T, LANE), F32),
                        pltpu.VMEM((D_MODEL // LANE, T + SUBLANES * SUBLANES, LANE), F32),
                        pltpu.VMEM((T + HALO_ROWS, D_MODEL), BF16),
                        pltpu.VMEM((T, D_MODEL), F32),
                        pltpu.VMEM((T + HALO_ROWS, D_MODEL), BF16),
                        pltpu.VMEM((T, D_MODEL), F32),
                        pltpu.VMEM((T + HALO_ROWS, D_MODEL), F32)],
        compiler_params=_cparams(("arbitrary",)),
        name="layer_tail",
    )(a, a, a, bs, bs, bs, att, att, att, *[src for src in srcs for _ in range(3)],
      mod, mod, wa, wb, wc, gpost, gpre, wu, cw, cb, wd, g)


def _zoh_blocks(lam_re, lam_im, log_dt, b_re, b_im, c_re, c_im):
    lr = jnp.minimum(lam_re, -1e-4)
    li = lam_im
    dt = jnp.exp(log_dt)[..., None]
    mag = jnp.exp(lr * dt)
    lbr = mag * jnp.cos(li * dt)
    lbi = mag * jnp.sin(li * dt)
    den = lr * lr + li * li
    fr = ((lbr - 1.0) * lr + lbi * li) / den
    fi = (lbi * lr - (lbr - 1.0) * li) / den
    bbr = fr[..., None] * b_re - fi[..., None] * b_im
    bbi = fr[..., None] * b_im + fi[..., None] * b_re
    eye = jnp.eye(SSM_GROUPS, dtype=F32)
    blk_b = lambda t: jnp.einsum('dgph,gk->dghkp', t, eye).reshape(2, SSM_WIDTH, SSM_LANES)
    blk_c = lambda t: jnp.einsum('dghp,gk->dgpkh', t, eye).reshape(2, SSM_LANES, SSM_WIDTH)
    bcat = jnp.concatenate([blk_b(bbr), blk_b(bbi)], axis=-1).astype(BF16)
    ccat = jnp.concatenate([blk_c(c_re), -blk_c(c_im)], axis=1).astype(BF16)
    lam = jnp.concatenate([lbr.reshape(2, 1, SSM_LANES), lbi.reshape(2, 1, SSM_LANES)], axis=-1)
    return bcat, lam, ccat


def _head_perm():
    cols = []
    for p in range(Q_PER_KV):
        for h in (p, Q_PER_KV + p):
            cols.extend(range(h * HEAD_DIM, (h + 1) * HEAD_DIM))
    return jnp.array(cols, dtype=jnp.int32)


def _rope_tables(n_ctx, n_lat):
    t = jnp.arange(n_lat)
    lane = jnp.arange(LANE)
    hl = lane % HEAD_DIM
    quarter = HEAD_DIM // 4
    inv_freq = ROPE_BASE ** (-jnp.arange(quarter, dtype=F32) / quarter)
    pos = jnp.where(hl[None, :] < HEAD_DIM // 2, (t // GRID_W)[:, None], (t % GRID_W)[:, None])
    ang = pos.astype(F32) * inv_freq[hl % quarter][None, :]
    first_half = (hl % (2 * quarter)) < quarter
    cos_l = jnp.cos(ang)
    sin_l = jnp.where(first_half[None, :], -jnp.sin(ang), jnp.sin(ang))
    cos_t = jnp.concatenate([jnp.ones((n_ctx, LANE), F32), cos_l], axis=0)
    sin_t = jnp.concatenate([jnp.zeros((n_ctx, LANE), F32), sin_l], axis=0)
    return cos_t, sin_t


def kernel(x, c, ctx, c_ctx, w_ada, b_ada, g_pre_mix, g_post_mix, g_pre_ffn, g_post_ffn, w_in, pool_w, pool_scale, lam_re, lam_im, log_dt, b_re, b_im, c_re, c_im, s5_d, w_glu, b_glu, sink, w_out, w_up, conv_w, conv_b, w_down):
    B, L, _ = x.shape
    N = ctx.shape[1]
    S = N + L
    assert N % TOK_TILE == 0 and L % TOK_TILE == 0 and L % GRID_W == 0

    rows = -(-(B + 1) // 8) * 8
    cond = jnp.concatenate([c, c_ctx[None, :], jnp.zeros((rows - B - 1, D_MODEL), F32)], axis=0)
    mod_all = _ada(cond, w_ada, b_ada)
    mod_lat = mod_all[:, :B].reshape(DEPTH, B, 1, 6, D_MODEL)
    mod_ctx = jnp.broadcast_to(mod_all[:, B].reshape(DEPTH, 1, 1, 6, D_MODEL), (DEPTH, B, 1, 6, D_MODEL))
    mods = jnp.concatenate([mod_ctx, mod_lat], axis=2)

    cos_t, sin_t = _rope_tables(N, L)
    perm = _head_perm()
    q0 = POOL_WIDTH + SSM_WIDTH
    in_cols = jnp.concatenate([jnp.arange(q0), q0 + perm, jnp.arange(q0 + ATTN_WIDTH, D_IN)])
    out_rows = jnp.concatenate([jnp.arange(q0), q0 + perm])
    eye_w = jnp.eye(len(POOL_WINDOWS), dtype=F32)
    nch = D_FF // FF_CHUNK

    def ffn_chunks(t):
        r = t.shape[0]
        return t.reshape(r, 2, nch, FF_CHUNK).transpose(2, 0, 1, 3).reshape(nch, r, 2 * FF_CHUNK)

    stream = (ctx, x)
    for l in range(DEPTH):
        last = l == DEPTH - 1
        w_in_l = w_in[l][:, in_cols].astype(BF16)
        w_out_l = w_out[l][out_rows].astype(BF16)
        pool_blk = jnp.einsum('gcd,gk->gckd', pool_w[l], eye_w).reshape(POOL_WIDTH, POOL_WIDTH).astype(BF16)
        bcat, lam, ccat = _zoh_blocks(lam_re[l], lam_im[l], log_dt[l], b_re[l], b_im[l], c_re[l], c_im[l])
        sink_b = jnp.broadcast_to(sink[l][:, None] * LOG2_E, (N_Q_HEADS, LANE))
        wu = ffn_chunks(w_up[l]).astype(BF16)
        cw = ffn_chunks(conv_w[l])
        cb = ffn_chunks(conv_b[l][None, :])
        wd = w_down[l].astype(BF16)

        pool_in, us, q, kt, v = _in_proj(*stream, S, mods[l], g_pre_mix[l][None, :], w_in_l, cos_t, sin_t,
                                         N // TOK_TILE)
        a = _pool(pool_in, pool_blk, pool_scale[l][None, :], N)
        bs = _s5(us.reshape(S * B, SSM_WIDTH), bcat, lam, ccat, s5_d[l][None, :],
                 w_glu[l].astype(BF16), b_glu[l][None, :], B, N)
        att = _attention(q, kt, v, sink_b, N, last)
        out = _layer_tail(a, bs.reshape(S, B * SSM_WIDTH), att, *stream, mods[l],
                          w_out_l[:POOL_WIDTH], w_out_l[POOL_WIDTH:q0], w_out_l[q0:],
                          g_post_mix[l][None, :], g_pre_ffn[l][None, :],
                          wu, cw, cb, wd, g_post_ffn[l][None, :], N, last)
        stream = (out, None)
    return out
```
